```python
import math
import jax, jax.numpy as jnp
from jax import lax
import numpy as np

D_MODEL = 1024
BATCH = 16
SEQ = 4096
DEPTH = 2
DEC_BATCH = 1
DEC_SEQ = 16384
PAST_LEN = 128

MIX_WIDTH = D_MODEL
FOURIER_WIDTH = D_MODEL // 2
N_FOURIER_GROUPS = 4
FOURIER_GROUP = FOURIER_WIDTH // N_FOURIER_GROUPS
HEAD_DIM = 64
N_HEADS = (MIX_WIDTH - FOURIER_WIDTH) // HEAD_DIM
N_KV_HEADS = 2
Q_PER_KV = N_HEADS // N_KV_HEADS
ATTN_WIDTH = N_HEADS * HEAD_DIM
KV_WIDTH = N_KV_HEADS * HEAD_DIM
IN_WIDTH = FOURIER_WIDTH + ATTN_WIDTH + 2 * KV_WIDTH
WINDOW = 128
BLOCK = 128
D_FF = 4 * D_MODEL
LN_EPS = 1e-5
DEEPNORM_ALPHA = (2.0 * DEPTH) ** 0.25
DEEPNORM_BETA = (8.0 * DEPTH) ** -0.25
NEG_BIG = -1e30

kernel_name = 'hymba_fnet_swa_alibi_deepnorm_encoder'


def layer_norm(x, g, b):
    xf = x.astype(jnp.float32)
    mu = jnp.mean(xf, axis=-1, keepdims=True)
    var = jnp.mean(jnp.square(xf - mu), axis=-1, keepdims=True)
    y = (xf - mu) * lax.rsqrt(var + LN_EPS) * g.astype(jnp.float32) + b.astype(jnp.float32)
    return y.astype(x.dtype)


def alibi_slopes():
    h = jnp.arange(N_HEADS, dtype=jnp.float32)
    return jnp.exp2(-8.0 * (h + 1.0) / N_HEADS)


def fourier_mix(u, w_fourier):
    B, S, _ = u.shape
    ug = u.reshape(B, S, N_FOURIER_GROUPS, FOURIER_GROUP).astype(jnp.float32)
    f = jnp.real(jnp.fft.fft2(ug, axes=(1, 3), norm='ortho')).astype(u.dtype)
    out = jnp.einsum('bsgc,gcd->bsgd', f, w_fourier)
    return out.reshape(B, S, FOURIER_WIDTH)


def windowed_attention(q, k, v, sink_logits):
    B, S, _ = q.shape
    nb = S // BLOCK
    qb = q.reshape(B, nb, BLOCK, N_KV_HEADS, Q_PER_KV, HEAD_DIM)

    def band(t):
        t = t.reshape(B, S, N_KV_HEADS, HEAD_DIM)
        tp = jnp.pad(t, ((0, 0), (WINDOW, WINDOW), (0, 0), (0, 0)))
        tp = tp.reshape(B, nb + 2, BLOCK, N_KV_HEADS, HEAD_DIM)
        return jnp.concatenate([tp[:, :-2], tp[:, 1:-1], tp[:, 2:]], axis=2)

    kb = band(k)
    vb = band(v)
    scale = 1.0 / math.sqrt(HEAD_DIM)
    scores = jnp.einsum('bnqkgd,bnskd->bnkgqs', qb, kb,
                        preferred_element_type=jnp.float32) * scale

    qi = jnp.arange(BLOCK)
    si = jnp.arange(3 * BLOCK)
    rel = si[None, :] - WINDOW - qi[:, None]
    dist = jnp.abs(rel).astype(jnp.float32)
    key_pos = jnp.arange(nb)[:, None] * BLOCK + si[None, :] - WINDOW
    valid = (jnp.abs(rel) <= WINDOW)[None, :, :] & ((key_pos >= 0) & (key_pos < S))[:, None, :]

    slopes = alibi_slopes().reshape(N_KV_HEADS, Q_PER_KV)
    scores = scores - slopes[:, :, None, None] * dist
    scores = jnp.where(valid[None, :, None, None], scores, NEG_BIG)

    sink = sink_logits.astype(jnp.float32).reshape(N_KV_HEADS, Q_PER_KV)[None, None, :, :, None, None]
    m = jnp.maximum(jnp.max(scores, axis=-1, keepdims=True), sink)
    p = jnp.exp(scores - m)
    denom = jnp.sum(p, axis=-1, keepdims=True) + jnp.exp(sink - m)
    probs = (p / denom).astype(v.dtype)
    out = jnp.einsum('bnkgqs,bnskd->bnqkgd', probs, vb)
    return out.reshape(B, S, ATTN_WIDTH)


def encoder_layer(x, w_in, w_fourier, sink_logits, w_out, ln_mix_g, ln_mix_b,
                  w_ff1, w_ff2, ln_ffn_g, ln_ffn_b):
    proj = x @ w_in
    c1 = FOURIER_WIDTH
    c2 = c1 + ATTN_WIDTH
    c3 = c2 + KV_WIDTH
    u = proj[..., :c1]
    q = proj[..., c1:c2]
    k = proj[..., c2:c3]
    v = proj[..., c3:]
    heads = jnp.concatenate([fourier_mix(u, w_fourier),
                             windowed_attention(q, k, v, sink_logits)], axis=-1)
    x = layer_norm(DEEPNORM_ALPHA * x + heads @ w_out, ln_mix_g, ln_mix_b)
    h = jnp.square(jax.nn.relu(x @ w_ff1))
    x = layer_norm(DEEPNORM_ALPHA * x + h @ w_ff2, ln_ffn_g, ln_ffn_b)
    return x


def trunk(x, ln_emb_g, ln_emb_b, w_in, w_fourier, sink_logits, w_out, ln_mix_g, ln_mix_b,
          w_ff1, w_ff2, ln_ffn_g, ln_ffn_b):
    x = layer_norm(x, ln_emb_g, ln_emb_b)
    for l in range(DEPTH):
        x = encoder_layer(x, w_in[l], w_fourier[l], sink_logits[l], w_out[l],
                          ln_mix_g[l], ln_mix_b[l], w_ff1[l], w_ff2[l],
                          ln_ffn_g[l], ln_ffn_b[l])
    return x


def setup_inputs(seed: int = 0) -> dict:
    key = jax.random.key(seed)
    ks = jax.random.split(key, 16)
    f32 = jnp.float32
    nrm = lambda k, shape, s: jax.random.normal(k, shape, f32) * s
    return {
        'x_prompt': nrm(ks[0], (BATCH, SEQ, D_MODEL), 1.0),
        'x_sample': nrm(ks[1], (DEC_BATCH, DEC_SEQ, D_MODEL), 1.0),
        'ln_emb_g': 1.0 + nrm(ks[2], (D_MODEL,), 0.02),
        'ln_emb_b': nrm(ks[3], (D_MODEL,), 0.02),
        'w_in': nrm(ks[4], (DEPTH, D_MODEL, IN_WIDTH), D_MODEL ** -0.5),
        'w_fourier': nrm(ks[5], (DEPTH, N_FOURIER_GROUPS, FOURIER_GROUP, FOURIER_GROUP), FOURIER_GROUP ** -0.5),
        'sink_logits': nrm(ks[6], (DEPTH, N_HEADS), 0.5),
        'w_out': nrm(ks[7], (DEPTH, MIX_WIDTH, D_MODEL), DEEPNORM_BETA * MIX_WIDTH ** -0.5),
        'ln_mix_g': 1.0 + nrm(ks[8], (DEPTH, D_MODEL), 0.02),
        'ln_mix_b': nrm(ks[9], (DEPTH, D_MODEL), 0.02),
        'w_ff1': nrm(ks[10], (DEPTH, D_MODEL, D_FF), D_MODEL ** -0.5),
        'w_ff2': nrm(ks[11], (DEPTH, D_FF, D_MODEL), DEEPNORM_BETA * D_FF ** -0.5),
        'ln_ffn_g': 1.0 + nrm(ks[12], (DEPTH, D_MODEL), 0.02),
        'ln_ffn_b': nrm(ks[13], (DEPTH, D_MODEL), 0.02),
    }


def reference(x_prompt, x_sample, ln_emb_g, ln_emb_b, w_in, w_fourier, sink_logits, w_out,
              ln_mix_g, ln_mix_b, w_ff1, w_ff2, ln_ffn_g, ln_ffn_b):
    y_prompt = trunk(x_prompt, ln_emb_g, ln_emb_b, w_in, w_fourier, sink_logits, w_out,
                     ln_mix_g, ln_mix_b, w_ff1, w_ff2, ln_ffn_g, ln_ffn_b)
    y_sample = trunk(x_sample, ln_emb_g, ln_emb_b, w_in, w_fourier, sink_logits, w_out,
                     ln_mix_g, ln_mix_b, w_ff1, w_ff2, ln_ffn_g, ln_ffn_b)
    return (y_prompt, y_sample)
```

```python
import functools
import math

import numpy as np
import jax
import jax.numpy as jnp
from jax import lax
from jax.experimental import pallas as pl
from jax.experimental.pallas import tpu as pltpu

F32 = jnp.float32
BF16 = jnp.bfloat16

FOURIER_WIDTH = 512
N_GROUPS = 4
GROUP = 128
HEAD_DIM = 64
N_HEADS = 8
N_KV = 2
Q_PER_KV = 4
ATTN_WIDTH = 512
KV_WIDTH = 128
WINDOW = 128
LN_EPS = 1e-5
NEG_BIG = -1e30
SUBLANES = 8
VMEM_LIMIT = 56 * 1024 * 1024

ROW_TILE = 512
Q_TILE = 512
FF_CHUNK = 1024


def _split(seq):
    lg = int(round(math.log2(seq)))
    assert 2 ** lg == seq
    s1 = 2 ** (lg // 2)
    s2 = seq // s1
    assert s1 % SUBLANES == 0 and s2 % SUBLANES == 0
    return s1, s2


def _const_spec(shape):
    n = len(shape)
    return pl.BlockSpec(shape, lambda *_: (0,) * n, pipeline_mode=pl.Buffered(1))


def _layer_norm(x, g, b):
    mu = jnp.mean(x, axis=-1, keepdims=True)
    xc = x - mu
    var = jnp.mean(xc * xc, axis=-1, keepdims=True)
    return xc * lax.rsqrt(var + LN_EPS) * g + b


def _channel_dft():
    k = np.arange(GROUP)
    ang = 2.0 * np.pi * np.outer(k, k) / GROUP
    cs = np.concatenate([np.cos(ang), -np.sin(ang)], axis=1) / np.sqrt(GROUP)
    return jnp.asarray(cs, F32).astype(BF16)


def _stage_matrices(seq):
    s1, s2 = _split(seq)
    eye = jnp.asarray(np.eye(SUBLANES), F32)
    k1 = np.arange(s1)
    ang1 = 2.0 * np.pi * np.outer(k1, k1) / s1
    c1 = np.cos(ang1) / np.sqrt(s1)
    sn1 = np.sin(ang1) / np.sqrt(s1)
    blk = np.stack([np.stack([c1, sn1], axis=1), np.stack([-sn1, c1], axis=1)], axis=0)
    a1 = jnp.einsum('pkqn,bc->pbkqnc', jnp.asarray(blk, F32), eye)
    a1 = a1.reshape(2 * SUBLANES * s1, 2 * s1 * SUBLANES).astype(BF16)

    n2 = np.arange(s2)
    anga = 2.0 * np.pi * np.outer(n2, n2) / s2
    ca = jnp.asarray(np.cos(anga) / np.sqrt(s2), F32)
    sa = jnp.asarray(np.sin(anga) / np.sqrt(s2), F32)
    angb = 2.0 * np.pi * np.outer(k1, n2) / seq
    cb = jnp.asarray(np.cos(angb), F32)
    sb = jnp.asarray(np.sin(angb), F32)
    cosphi = ca[None, :, :] * cb[:, None, :] - sa[None, :, :] * sb[:, None, :]
    sinphi = sa[None, :, :] * cb[:, None, :] + ca[None, :, :] * sb[:, None, :]
    trig = jnp.stack([cosphi, sinphi], axis=0).reshape(2, s1 // SUBLANES, SUBLANES, s2, s2)
    a2 = jnp.einsum('qibkn,bc->ikbqnc', trig, eye)
    a2 = a2.reshape(s1 // SUBLANES, s2 * SUBLANES, 2 * s2 * SUBLANES).astype(BF16)
    return a1, a2


def _attn_bias():
    q = np.arange(WINDOW)[:, None]
    s = np.arange(3 * WINDOW)[None, :]
    rel = s - WINDOW - q
    slopes = 2.0 ** (-8.0 * (np.arange(N_HEADS) + 1.0) / N_HEADS)
    out = np.empty((N_KV, Q_PER_KV * WINDOW, 3 * WINDOW), np.float32)
    for h in range(N_HEADS):
        b = np.where(np.abs(rel) <= WINDOW, -slopes[h] * np.abs(rel), NEG_BIG)
        kh, g = divmod(h, Q_PER_KV)
        out[kh, g * WINDOW:(g + 1) * WINDOW] = b
    return jnp.asarray(out)


def _proj_body(first, x_ref, eg_ref, eb_ref, w_ref, u_ref, q_ref, k_ref, v_ref):
    x = x_ref[...]
    if first:
        x = _layer_norm(x, eg_ref[...], eb_ref[...])
    p = jnp.dot(x.astype(BF16), w_ref[...], preferred_element_type=F32)
    c1 = FOURIER_WIDTH
    c2 = c1 + ATTN_WIDTH
    c3 = c2 + 2 * KV_WIDTH
    u_ref[...] = p[:, :c1]
    q_ref[...] = (p[:, c1:c2] * (1.0 / math.sqrt(HEAD_DIM))).astype(BF16)
    k_ref[...] = p[:, c2:c3].astype(BF16)
    v_ref[...] = p[:, c3:].astype(BF16)


def _proj(x, eg, eb, w_ext, first):
    bsz, seq, d = x.shape
    tm = min(ROW_TILE, seq)
    wn = w_ext.shape[1]
    row = lambda width: pl.BlockSpec((None, tm, width), lambda b, t: (b, t, 0))
    return pl.pallas_call(
        functools.partial(_proj_body, first),
        grid=(bsz, seq // tm),
        in_specs=[row(d), _const_spec((1, d)), _const_spec((1, d)), _const_spec((d, wn))],
        out_specs=[row(FOURIER_WIDTH), row(ATTN_WIDTH), row(2 * KV_WIDTH), row(2 * KV_WIDTH)],
        out_shape=[jax.ShapeDtypeStruct((bsz, seq, FOURIER_WIDTH), F32),
                   jax.ShapeDtypeStruct((bsz, seq, ATTN_WIDTH), BF16),
                   jax.ShapeDtypeStruct((bsz, seq, 2 * KV_WIDTH), BF16),
                   jax.ShapeDtypeStruct((bsz, seq, 2 * KV_WIDTH), BF16)],
        compiler_params=pltpu.CompilerParams(vmem_limit_bytes=VMEM_LIMIT),
        name="proj",
    )(x, eg, eb, w_ext)


def _f1_body(u_ref, cs_ref, a1_ref, y_ref):
    s1 = u_ref.shape[0]
    r = s1 * SUBLANES
    u = u_ref[...].reshape(r, FOURIER_WIDTH).astype(BF16)
    zr, zi = [], []
    for g in range(N_GROUPS):
        z = jnp.dot(u[:, g * GROUP:(g + 1) * GROUP], cs_ref[...], preferred_element_type=F32)
        zr.append(z[:, :GROUP])
        zi.append(z[:, GROUP:])
    zs = jnp.concatenate([jnp.concatenate(zr, axis=1), jnp.concatenate(zi, axis=1)], axis=0).astype(BF16)
    y = jnp.dot(a1_ref[...], zs, preferred_element_type=F32)
    y_ref[:, :FOURIER_WIDTH] = y[:r]
    y_ref[:, FOURIER_WIDTH:] = y[r:]


def _f1(u, cs, a1):
    bsz, seq, _ = u.shape
    s1, s2 = _split(seq)
    nj = s2 // SUBLANES
    r = s1 * SUBLANES
    u5 = u.reshape(bsz, s1, nj, SUBLANES, FOURIER_WIDTH)
    return pl.pallas_call(
        _f1_body,
        grid=(bsz, nj),
        in_specs=[pl.BlockSpec((None, s1, None, SUBLANES, FOURIER_WIDTH), lambda b, j: (b, 0, j, 0, 0)),
                  _const_spec(cs.shape), _const_spec(a1.shape)],
        out_specs=pl.BlockSpec((None, r, 2 * FOURIER_WIDTH), lambda b, j: (b, j, 0)),
        out_shape=jax.ShapeDtypeStruct((bsz, seq, 2 * FOURIER_WIDTH), F32),
        compiler_params=pltpu.CompilerParams(vmem_limit_bytes=VMEM_LIMIT),
        name="fourier_stage1",
    )(u5, cs, a1)


def _f2_body(y_ref, a2_ref, wf_ref, o_ref):
    s2 = y_ref.shape[0]
    r = s2 * SUBLANES
    y = y_ref[...].reshape(r, 2 * FOURIER_WIDTH)
    ys = jnp.concatenate([y[:, :FOURIER_WIDTH], y[:, FOURIER_WIDTH:]], axis=0).astype(BF16)
    f = jnp.dot(a2_ref[...], ys, preferred_element_type=F32).astype(BF16)
    outs = [jnp.dot(f[:, g * GROUP:(g + 1) * GROUP], wf_ref[g], preferred_element_type=F32)
            for g in range(N_GROUPS)]
    o_ref[...] = jnp.concatenate(outs, axis=1).reshape(s2, SUBLANES, FOURIER_WIDTH)


def _f2(y, a2, wf):
    bsz, seq, _ = y.shape
    s1, s2 = _split(seq)
    ni = s1 // SUBLANES
    r = s2 * SUBLANES
    y5 = y.reshape(bsz, s2, ni, SUBLANES, 2 * FOURIER_WIDTH)
    out = pl.pallas_call(
        _f2_body,
        grid=(bsz, ni),
        in_specs=[pl.BlockSpec((None, s2, None, SUBLANES, 2 * FOURIER_WIDTH), lambda b, i: (b, 0, i, 0, 0)),
                  pl.BlockSpec((None, r, 2 * r), lambda b, i: (i, 0, 0)),
                  _const_spec(wf.shape)],
        out_specs=pl.BlockSpec((None, s2, None, SUBLANES, FOURIER_WIDTH), lambda b, i: (b, 0, i, 0, 0)),
        out_shape=jax.ShapeDtypeStruct((bsz, s2, ni, SUBLANES, FOURIER_WIDTH), F32),
        compiler_params=pltpu.CompilerParams(vmem_limit_bytes=VMEM_LIMIT),
        name="fourier_stage2",
    )(y5, a2, wf)
    return out.reshape(bsz, seq, FOURIER_WIDTH)


def _attn_body(nsub, q_ref, kp_ref, kc_ref, kn_ref, vp_ref, vc_ref, vn_ref, bias_ref, sink_ref, o_ref):
    t = pl.program_id(1)
    last = pl.num_programs(1) - 1
    kcat = jnp.concatenate([kp_ref[...], kc_ref[...], kn_ref[...]], axis=0)
    vcat = jnp.concatenate([vp_ref[...], vc_ref[...], vn_ref[...]], axis=0)
    neg_first = jnp.where(t == 0, NEG_BIG, 0.0).astype(F32)
    neg_last = jnp.where(t == last, NEG_BIG, 0.0).astype(F32)
    lane = lax.broadcasted_iota(jnp.int32, (WINDOW, 2 * HEAD_DIM), 1)
    left = lane < HEAD_DIM
    m_left = left.astype(BF16)
    m_right = 1 - m_left
    for s in range(nsub):
        r0 = s * WINDOW
        qs = q_ref[r0:r0 + WINDOW, :]
        for kh in range(N_KV):
            c0 = kh * 2 * GROUP
            qa = qs[:, c0:c0 + GROUP]
            qb = qs[:, c0 + GROUP:c0 + 2 * GROUP]
            st = jnp.concatenate([qa * m_left, qa * m_right, qb * m_left, qb * m_right], axis=0)
            ks = kcat[r0:r0 + 3 * WINDOW, kh * GROUP:(kh + 1) * GROUP]
            vs = vcat[r0:r0 + 3 * WINDOW, kh * GROUP:(kh + 1) * GROUP]
            sc = lax.dot_general(st, ks, (((1,), (1,)), ((), ())), preferred_element_type=F32)
            sc = sc + bias_ref[kh]
            if s == 0:
                sc = jnp.concatenate([sc[:, :WINDOW] + neg_first, sc[:, WINDOW:]], axis=1)
            if s == nsub - 1:
                sc = jnp.concatenate([sc[:, :2 * WINDOW], sc[:, 2 * WINDOW:] + neg_last], axis=1)
            sink = sink_ref[kh]
            m = jnp.maximum(jnp.max(sc, axis=1, keepdims=True), sink)
            p = jnp.exp(sc - m)
            denom = jnp.sum(p, axis=1, keepdims=True) + jnp.exp(sink - m)
            pv = jnp.dot(p.astype(BF16), vs, preferred_element_type=F32)
            pv = pv * (1.0 / denom)
            o_ref[r0:r0 + WINDOW, c0:c0 + GROUP] = jnp.where(left, pv[:WINDOW], pv[WINDOW:2 * WINDOW])
            o_ref[r0:r0 + WINDOW, c0 + GROUP:c0 + 2 * GROUP] = jnp.where(
                left, pv[2 * WINDOW:3 * WINDOW], pv[3 * WINDOW:])


def _attn(q, k2, v2, bias, sink_col):
    bsz, seq, _ = q.shape
    tq = min(Q_TILE, seq)
    nsub = tq // WINDOW
    nblk = seq // WINDOW
    kvw = 2 * KV_WIDTH
    prev = pl.BlockSpec((None, WINDOW, kvw), lambda b, t: (b, jnp.maximum(t * nsub - 1, 0), 0))
    cur = pl.BlockSpec((None, tq, kvw), lambda b, t: (b, t, 0))
    nxt = pl.BlockSpec((None, WINDOW, kvw), lambda b, t: (b, jnp.minimum((t + 1) * nsub, nblk - 1), 0))
    return pl.pallas_call(
        functools.partial(_attn_body, nsub),
        grid=(bsz, seq // tq),
        in_specs=[pl.BlockSpec((None, tq, ATTN_WIDTH), lambda b, t: (b, t, 0)),
                  prev, cur, nxt, prev, cur, nxt,
                  _const_spec(bias.shape), _const_spec(sink_col.shape)],
        out_specs=pl.BlockSpec((None, tq, ATTN_WIDTH), lambda b, t: (b, t, 0)),
        out_shape=jax.ShapeDtypeStruct((bsz, seq, ATTN_WIDTH), F32),
        compiler_params=pltpu.CompilerParams(vmem_limit_bytes=VMEM_LIMIT),
        name="window_attention",
    )(q, k2, k2, k2, v2, v2, v2, bias, sink_col)


def _post_body(first, alpha, x_ref, fo_ref, ao_ref, eg_ref, eb_ref, wo_ref, g1_ref, b1_ref,
               w1_ref, w2_ref, g2_ref, b2_ref, o_ref):
    x = x_ref[...]
    if first:
        x = _layer_norm(x, eg_ref[...], eb_ref[...])
    heads = jnp.concatenate([fo_ref[...], ao_ref[...]], axis=1).astype(BF16)
    mix = jnp.dot(heads, wo_ref[...], preferred_element_type=F32)
    x1 = _layer_norm(alpha * x + mix, g1_ref[...], b1_ref[...])
    x1b = x1.astype(BF16)
    dff = w1_ref.shape[1]
    z = None
    for c in range(dff // FF_CHUNK):
        h = jnp.dot(x1b, w1_ref[:, c * FF_CHUNK:(c + 1) * FF_CHUNK], preferred_element_type=F32)
        h = jnp.square(jnp.maximum(h, 0.0)).astype(BF16)
        zc = jnp.dot(h, w2_ref[c * FF_CHUNK:(c + 1) * FF_CHUNK, :], preferred_element_type=F32)
        z = zc if z is None else z + zc
    o_ref[...] = _layer_norm(alpha * x1 + z, g2_ref[...], b2_ref[...])


def _post(x, fo, ao, eg, eb, wo, g1, b1, w1, w2, g2, b2, first, alpha):
    bsz, seq, d = x.shape
    tm = min(ROW_TILE, seq)
    dff = w1.shape[1]
    row = lambda width: pl.BlockSpec((None, tm, width), lambda b, t: (b, t, 0))
    vec = _const_spec((1, d))
    return pl.pallas_call(
        functools.partial(_post_body, first, alpha),
        grid=(bsz, seq // tm),
        in_specs=[row(d), row(FOURIER_WIDTH), row(ATTN_WIDTH), vec, vec,
                  _const_spec((d, d)), vec, vec, _const_spec((d, dff)), _const_spec((dff, d)), vec, vec],
        out_specs=row(d),
        out_shape=jax.ShapeDtypeStruct((bsz, seq, d), F32),
        compiler_params=pltpu.CompilerParams(vmem_limit_bytes=VMEM_LIMIT),
        name="mix_mlp",
    )(x, fo, ao, eg, eb, wo, g1, b1, w1, w2, g2, b2)


def _trunk(x, consts, params):
    cs, a1, a2, bias = consts
    depth = params['w_in'].shape[0]
    alpha = (2.0 * depth) ** 0.25
    eg = params['ln_emb_g'].reshape(1, -1)
    eb = params['ln_emb_b'].reshape(1, -1)
    for l in range(depth):
        first = l == 0
        w_in = params['w_in'][l]
        c2 = FOURIER_WIDTH + ATTN_WIDTH
        kcols = [w_in[:, c2 + h * HEAD_DIM:c2 + (h + 1) * HEAD_DIM] for h in range(N_KV)]
        vcols = [w_in[:, c2 + KV_WIDTH + h * HEAD_DIM:c2 + KV_WIDTH + (h + 1) * HEAD_DIM] for h in range(N_KV)]
        w_ext = jnp.concatenate([w_in[:, :c2]] + [kcols[0]] * 2 + [kcols[1]] * 2
                                + [vcols[0]] * 2 + [vcols[1]] * 2, axis=1).astype(BF16)
        sink_col = jnp.repeat(params['sink_logits'][l].astype(F32), WINDOW).reshape(N_KV, Q_PER_KV * WINDOW, 1)
        u, q, k2, v2 = _proj(x, eg, eb, w_ext, first)
        y = _f1(u, cs, a1)
        fo = _f2(y, a2, params['w_fourier'][l].astype(BF16))
        ao = _attn(q, k2, v2, bias, sink_col)
        x = _post(x, fo, ao, eg, eb,
                  params['w_out'][l].astype(BF16),
                  params['ln_mix_g'][l].reshape(1, -1), params['ln_mix_b'][l].reshape(1, -1),
                  params['w_ff1'][l].astype(BF16), params['w_ff2'][l].astype(BF16),
                  params['ln_ffn_g'][l].reshape(1, -1), params['ln_ffn_b'][l].reshape(1, -1),
                  first, alpha)
    return x


def kernel(x_prompt, x_sample, ln_emb_g, ln_emb_b, w_in, w_fourier, sink_logits, w_out,
           ln_mix_g, ln_mix_b, w_ff1, w_ff2, ln_ffn_g, ln_ffn_b):
    params = dict(ln_emb_g=ln_emb_g, ln_emb_b=ln_emb_b, w_in=w_in, w_fourier=w_fourier,
                  sink_logits=sink_logits, w_out=w_out, ln_mix_g=ln_mix_g, ln_mix_b=ln_mix_b,
                  w_ff1=w_ff1, w_ff2=w_ff2, ln_ffn_g=ln_ffn_g, ln_ffn_b=ln_ffn_b)
    cs = _channel_dft()
    bias = _attn_bias()
    outs = []
    for x in (x_prompt, x_sample):
        a1, a2 = _stage_matrices(x.shape[1])
        outs.append(_trunk(x, (cs, a1, a2, bias), params))
    return tuple(outs)
```

```python
import functools
import math

import numpy as np
import jax
import jax.numpy as jnp
from jax import lax
from jax.experimental import pallas as pl
from jax.experimental.pallas import tpu as pltpu

F32 = jnp.float32
BF16 = jnp.bfloat16

FOURIER_WIDTH = 512
N_GROUPS = 4
GROUP = 128
HEAD_DIM = 64
N_HEADS = 8
N_KV = 2
Q_PER_KV = 4
ATTN_WIDTH = 512
KV_WIDTH = 128
WINDOW = 128
LN_EPS = 1e-5
NEG_BIG = -1e30
SUBLANES = 8
VMEM_LIMIT = 56 * 1024 * 1024

ROW_TILE = 512
Q_TILE = 512
FF_CHUNK = 1024


def _split(seq):
    lg = int(round(math.log2(seq)))
    assert 2 ** lg == seq
    s1 = 2 ** (lg // 2)
    s2 = seq // s1
    assert s1 % SUBLANES == 0 and s2 % SUBLANES == 0
    return s1, s2


def _const_spec(shape):
    n = len(shape)
    return pl.BlockSpec(shape, lambda *_: (0,) * n, pipeline_mode=pl.Buffered(1))


def _layer_norm(x, g, b):
    mu = jnp.mean(x, axis=-1, keepdims=True)
    xc = x - mu
    var = jnp.mean(xc * xc, axis=-1, keepdims=True)
    return xc * lax.rsqrt(var + LN_EPS) * g + b


def _channel_dft():
    k = np.arange(GROUP)
    ang = 2.0 * np.pi * np.outer(k, k) / GROUP
    cs = np.concatenate([np.cos(ang), -np.sin(ang)], axis=1) / np.sqrt(GROUP)
    return jnp.asarray(cs, F32).astype(BF16)


def _stage_matrices(seq):
    s1, s2 = _split(seq)
    i32 = jnp.int32
    sub = jnp.arange(SUBLANES, dtype=i32)

    l1 = 2 * s1 * SUBLANES
    col = jnp.arange(l1, dtype=i32)
    part = col // (s1 * SUBLANES)
    n1c = (col // SUBLANES) % s1
    cc = col % SUBLANES
    k1 = jnp.arange(s1, dtype=i32)
    ang = ((k1[:, None] * n1c[None, :]) % s1).astype(F32) * (2.0 * math.pi / s1)
    c = jnp.cos(ang) * (1.0 / math.sqrt(s1))
    s = jnp.sin(ang) * (1.0 / math.sqrt(s1))
    blk = jnp.stack([jnp.where(part == 0, c, s), jnp.where(part == 0, -s, c)])
    mask = (cc[None, :] == sub[:, None]).astype(F32)
    a1 = (blk[:, None, :, :] * mask[None, :, None, :]).reshape(2 * SUBLANES * s1, l1).astype(BF16)

    l2 = 2 * s2 * SUBLANES
    col = jnp.arange(l2, dtype=i32)
    part = col // (s2 * SUBLANES)
    n2c = (col // SUBLANES) % s2
    cc = col % SUBLANES
    k2 = jnp.arange(s2, dtype=i32)
    anga = ((k2[:, None] * n2c[None, :]) % s2).astype(F32) * (2.0 * math.pi / s2)
    ca = jnp.cos(anga) * (1.0 / math.sqrt(s2))
    sa = jnp.sin(anga) * (1.0 / math.sqrt(s2))
    pm = jnp.where(part == 0, ca, sa)
    qm = jnp.where(part == 0, -sa, ca)
    angb = ((k1[:, None] * n2c[None, :]) % seq).astype(F32) * (2.0 * math.pi / seq)
    keep = cc[None, :] == (k1 % SUBLANES)[:, None]
    cbm = jnp.where(keep, jnp.cos(angb), 0.0).reshape(s1 // SUBLANES, 1, SUBLANES, l2)
    sbm = jnp.where(keep, jnp.sin(angb), 0.0).reshape(s1 // SUBLANES, 1, SUBLANES, l2)
    a2 = pm[None, :, None, :] * cbm + qm[None, :, None, :] * sbm
    a2 = a2.reshape(s1 // SUBLANES, s2 * SUBLANES, l2).astype(BF16)
    return a1, a2


def _attn_bias():
    q = np.arange(WINDOW)[None, :]
    s = np.arange(3 * WINDOW)[:, None]
    rel = s - WINDOW - q
    slopes = 2.0 ** (-8.0 * (np.arange(N_HEADS) + 1.0) / N_HEADS)
    out = np.empty((N_KV, 3 * WINDOW, Q_PER_KV * WINDOW), np.float32)
    for h in range(N_HEADS):
        b = np.where(np.abs(rel) <= WINDOW, -slopes[h] * np.abs(rel), NEG_BIG)
        kh, g = divmod(h, Q_PER_KV)
        out[kh, :, g * WINDOW:(g + 1) * WINDOW] = b
    return jnp.asarray(out)


def _proj_body(first, x_ref, eg_ref, eb_ref, w_ref, wvt_ref, u_ref, q_ref, k_ref, vt_ref):
    x = x_ref[...]
    if first:
        x = _layer_norm(x, eg_ref[...], eb_ref[...])
    xb = x.astype(BF16)
    p = jnp.dot(xb, w_ref[...], preferred_element_type=F32)
    c1 = FOURIER_WIDTH
    c2 = c1 + ATTN_WIDTH
    u_ref[...] = p[:, :c1]
    q_ref[...] = (p[:, c1:c2] * (1.0 / math.sqrt(HEAD_DIM))).astype(BF16)
    k_ref[...] = p[:, c2:].astype(BF16)
    vt = lax.dot_general(wvt_ref[...], xb, (((1,), (1,)), ((), ())), preferred_element_type=F32)
    vt_ref[...] = vt.astype(BF16)


def _proj(x, eg, eb, w_ext, wvt, first):
    bsz, seq, d = x.shape
    tm = min(ROW_TILE, seq)
    wn = w_ext.shape[1]
    row = lambda width: pl.BlockSpec((None, tm, width), lambda b, t: (b, t, 0))
    return pl.pallas_call(
        functools.partial(_proj_body, first),
        grid=(bsz, seq // tm),
        in_specs=[row(d), _const_spec((1, d)), _const_spec((1, d)), _const_spec((d, wn)),
                  _const_spec((KV_WIDTH, d))],
        out_specs=[row(FOURIER_WIDTH), row(ATTN_WIDTH), row(2 * KV_WIDTH),
                   pl.BlockSpec((None, KV_WIDTH, tm), lambda b, t: (b, 0, t))],
        out_shape=[jax.ShapeDtypeStruct((bsz, seq, FOURIER_WIDTH), F32),
                   jax.ShapeDtypeStruct((bsz, seq, ATTN_WIDTH), BF16),
                   jax.ShapeDtypeStruct((bsz, seq, 2 * KV_WIDTH), BF16),
                   jax.ShapeDtypeStruct((bsz, KV_WIDTH, seq), BF16)],
        compiler_params=pltpu.CompilerParams(vmem_limit_bytes=VMEM_LIMIT),
        name="proj",
    )(x, eg, eb, w_ext, wvt)


def _f1_body(u_ref, cs_ref, a1_ref, y_ref):
    s1 = u_ref.shape[0]
    r = s1 * SUBLANES
    u = u_ref[...].reshape(r, FOURIER_WIDTH).astype(BF16)
    zr, zi = [], []
    for g in range(N_GROUPS):
        z = jnp.dot(u[:, g * GROUP:(g + 1) * GROUP], cs_ref[...], preferred_element_type=F32)
        zr.append(z[:, :GROUP])
        zi.append(z[:, GROUP:])
    zs = jnp.concatenate([jnp.concatenate(zr, axis=1), jnp.concatenate(zi, axis=1)], axis=0).astype(BF16)
    y = jnp.dot(a1_ref[...], zs, preferred_element_type=F32)
    y_ref[:, :FOURIER_WIDTH] = y[:r]
    y_ref[:, FOURIER_WIDTH:] = y[r:]


def _f1(u, cs, a1):
    bsz, seq, _ = u.shape
    s1, s2 = _split(seq)
    nj = s2 // SUBLANES
    r = s1 * SUBLANES
    u5 = u.reshape(bsz, s1, nj, SUBLANES, FOURIER_WIDTH)
    return pl.pallas_call(
        _f1_body,
        grid=(bsz, nj),
        in_specs=[pl.BlockSpec((None, s1, None, SUBLANES, FOURIER_WIDTH), lambda b, j: (b, 0, j, 0, 0)),
                  _const_spec(cs.shape), _const_spec(a1.shape)],
        out_specs=pl.BlockSpec((None, r, 2 * FOURIER_WIDTH), lambda b, j: (b, j, 0)),
        out_shape=jax.ShapeDtypeStruct((bsz, seq, 2 * FOURIER_WIDTH), F32),
        compiler_params=pltpu.CompilerParams(vmem_limit_bytes=VMEM_LIMIT),
        name="fourier_stage1",
    )(u5, cs, a1)


def _f2_body(y_ref, a2_ref, wf_ref, o_ref):
    s2 = y_ref.shape[0]
    r = s2 * SUBLANES
    y = y_ref[...].reshape(r, 2 * FOURIER_WIDTH)
    ys = jnp.concatenate([y[:, :FOURIER_WIDTH], y[:, FOURIER_WIDTH:]], axis=0).astype(BF16)
    f = jnp.dot(a2_ref[...], ys, preferred_element_type=F32).astype(BF16)
    outs = [jnp.dot(f[:, g * GROUP:(g + 1) * GROUP], wf_ref[g], preferred_element_type=F32)
            for g in range(N_GROUPS)]
    o_ref[...] = jnp.concatenate(outs, axis=1).reshape(s2, SUBLANES, FOURIER_WIDTH)


def _f2(y, a2, wf):
    bsz, seq, _ = y.shape
    s1, s2 = _split(seq)
    ni = s1 // SUBLANES
    r = s2 * SUBLANES
    y5 = y.reshape(bsz, s2, ni, SUBLANES, 2 * FOURIER_WIDTH)
    out = pl.pallas_call(
        _f2_body,
        grid=(bsz, ni),
        in_specs=[pl.BlockSpec((None, s2, None, SUBLANES, 2 * FOURIER_WIDTH), lambda b, i: (b, 0, i, 0, 0)),
                  pl.BlockSpec((None, r, 2 * r), lambda b, i: (i, 0, 0)),
                  _const_spec(wf.shape)],
        out_specs=pl.BlockSpec((None, s2, None, SUBLANES, FOURIER_WIDTH), lambda b, i: (b, 0, i, 0, 0)),
        out_shape=jax.ShapeDtypeStruct((bsz, s2, ni, SUBLANES, FOURIER_WIDTH), F32),
        compiler_params=pltpu.CompilerParams(vmem_limit_bytes=VMEM_LIMIT),
        name="fourier_stage2",
    )(y5, a2, wf)
    return out.reshape(bsz, seq, FOURIER_WIDTH)


def _attn_body(nsub, q_ref, kp_ref, kc_ref, kn_ref, vp_ref, vc_ref, vn_ref, bias_ref, sink_ref, o_ref):
    t = pl.program_id(1)
    last = pl.num_programs(1) - 1
    kcat = jnp.concatenate([kp_ref[...], kc_ref[...], kn_ref[...]], axis=0)
    vtcat = jnp.concatenate([vp_ref[...], vc_ref[...], vn_ref[...]], axis=1)
    neg_first = jnp.where(t == 0, NEG_BIG, 0.0).astype(F32)
    neg_last = jnp.where(t == last, NEG_BIG, 0.0).astype(F32)
    lane = lax.broadcasted_iota(jnp.int32, (WINDOW, 2 * HEAD_DIM), 1)
    m_left = (lane < HEAD_DIM).astype(BF16)
    m_right = 1 - m_left
    for s in range(nsub):
        r0 = s * WINDOW
        qs = q_ref[r0:r0 + WINDOW, :]
        for kh in range(N_KV):
            c0 = kh * 2 * GROUP
            qa = qs[:, c0:c0 + GROUP]
            qb = qs[:, c0 + GROUP:c0 + 2 * GROUP]
            st = jnp.concatenate([qa * m_left, qa * m_right, qb * m_left, qb * m_right], axis=0)
            ks = kcat[r0:r0 + 3 * WINDOW, kh * GROUP:(kh + 1) * GROUP]
            sc = lax.dot_general(ks, st, (((1,), (1,)), ((), ())), preferred_element_type=F32)
            sc = sc + bias_ref[kh]
            if s == 0:
                sc = jnp.concatenate([sc[:WINDOW] + neg_first, sc[WINDOW:]], axis=0)
            if s == nsub - 1:
                sc = jnp.concatenate([sc[:2 * WINDOW], sc[2 * WINDOW:] + neg_last], axis=0)
            sink = sink_ref[kh]
            m = jnp.maximum(jnp.max(sc, axis=0, keepdims=True), sink)
            p = jnp.exp(sc - m)
            denom = jnp.sum(p, axis=0, keepdims=True) + jnp.exp(sink - m)
            pn = (p * (1.0 / denom)).astype(BF16)
            vts = vtcat[kh * HEAD_DIM:(kh + 1) * HEAD_DIM, r0:r0 + 3 * WINDOW]
            ot = jnp.dot(vts, pn, preferred_element_type=F32)
            for pair in range(Q_PER_KV // 2):
                two = jnp.concatenate([ot[:, (2 * pair) * WINDOW:(2 * pair + 1) * WINDOW],
                                       ot[:, (2 * pair + 1) * WINDOW:(2 * pair + 2) * WINDOW]], axis=0)
                o_ref[r0:r0 + WINDOW, c0 + pair * GROUP:c0 + (pair + 1) * GROUP] = two.T


def _attn(q, k2, vt, bias, sink_row):
    bsz, seq, _ = q.shape
    tq = min(Q_TILE, seq)
    nsub = tq // WINDOW
    nblk = seq // WINDOW
    kvw = 2 * KV_WIDTH
    kprev = pl.BlockSpec((None, WINDOW, kvw), lambda b, t: (b, jnp.maximum(t * nsub - 1, 0), 0))
    kcur = pl.BlockSpec((None, tq, kvw), lambda b, t: (b, t, 0))
    knext = pl.BlockSpec((None, WINDOW, kvw), lambda b, t: (b, jnp.minimum((t + 1) * nsub, nblk - 1), 0))
    vprev = pl.BlockSpec((None, KV_WIDTH, WINDOW), lambda b, t: (b, 0, jnp.maximum(t * nsub - 1, 0)))
    vcur = pl.BlockSpec((None, KV_WIDTH, tq), lambda b, t: (b, 0, t))
    vnext = pl.BlockSpec((None, KV_WIDTH, WINDOW), lambda b, t: (b, 0, jnp.minimum((t + 1) * nsub, nblk - 1)))
    return pl.pallas_call(
        functools.partial(_attn_body, nsub),
        grid=(bsz, seq // tq),
        in_specs=[pl.BlockSpec((None, tq, ATTN_WIDTH), lambda b, t: (b, t, 0)),
                  kprev, kcur, knext, vprev, vcur, vnext,
                  _const_spec(bias.shape), _const_spec(sink_row.shape)],
        out_specs=pl.BlockSpec((None, tq, ATTN_WIDTH), lambda b, t: (b, t, 0)),
        out_shape=jax.ShapeDtypeStruct((bsz, seq, ATTN_WIDTH), F32),
        compiler_params=pltpu.CompilerParams(vmem_limit_bytes=VMEM_LIMIT),
        name="window_attention",
    )(q, k2, k2, k2, vt, vt, vt, bias, sink_row)


def _post_body(first, alpha, x_ref, fo_ref, ao_ref, eg_ref, eb_ref, wo_ref, g1_ref, b1_ref,
               w1_ref, w2_ref, g2_ref, b2_ref, o_ref):
    x = x_ref[...]
    if first:
        x = _layer_norm(x, eg_ref[...], eb_ref[...])
    heads = jnp.concatenate([fo_ref[...], ao_ref[...]], axis=1).astype(BF16)
    mix = jnp.dot(heads, wo_ref[...], preferred_element_type=F32)
    x1 = _layer_norm(alpha * x + mix, g1_ref[...], b1_ref[...])
    x1b = x1.astype(BF16)
    dff = w1_ref.shape[1]
    z = None
    for c in range(dff // FF_CHUNK):
        h = jnp.dot(x1b, w1_ref[:, c * FF_CHUNK:(c + 1) * FF_CHUNK], preferred_element_type=F32)
        h = jnp.square(jnp.maximum(h, 0.0)).astype(BF16)
        zc = jnp.dot(h, w2_ref[c * FF_CHUNK:(c + 1) * FF_CHUNK, :], preferred_element_type=F32)
        z = zc if z is None else z + zc
    o_ref[...] = _layer_norm(alpha * x1 + z, g2_ref[...], b2_ref[...])


def _post(x, fo, ao, eg, eb, wo, g1, b1, w1, w2, g2, b2, first, alpha):
    bsz, seq, d = x.shape
    tm = min(ROW_TILE, seq)
    dff = w1.shape[1]
    row = lambda width: pl.BlockSpec((None, tm, width), lambda b, t: (b, t, 0))
    vec = _const_spec((1, d))
    return pl.pallas_call(
        functools.partial(_post_body, first, alpha),
        grid=(bsz, seq // tm),
        in_specs=[row(d), row(FOURIER_WIDTH), row(ATTN_WIDTH), vec, vec,
                  _const_spec((d, d)), vec, vec, _const_spec((d, dff)), _const_spec((dff, d)), vec, vec],
        out_specs=row(d),
        out_shape=jax.ShapeDtypeStruct((bsz, seq, d), F32),
        compiler_params=pltpu.CompilerParams(vmem_limit_bytes=VMEM_LIMIT),
        name="mix_mlp",
    )(x, fo, ao, eg, eb, wo, g1, b1, w1, w2, g2, b2)


def _trunk(x, consts, params):
    cs, a1, a2, bias = consts
    depth = params['w_in'].shape[0]
    alpha = (2.0 * depth) ** 0.25
    eg = params['ln_emb_g'].reshape(1, -1)
    eb = params['ln_emb_b'].reshape(1, -1)
    for l in range(depth):
        first = l == 0
        w_in = params['w_in'][l]
        c2 = FOURIER_WIDTH + ATTN_WIDTH
        kcols = [w_in[:, c2 + h * HEAD_DIM:c2 + (h + 1) * HEAD_DIM] for h in range(N_KV)]
        w_ext = jnp.concatenate([w_in[:, :c2]] + [kcols[0]] * 2 + [kcols[1]] * 2, axis=1).astype(BF16)
        wvt = w_in[:, c2 + KV_WIDTH:].T.astype(BF16)
        sink_row = jnp.repeat(params['sink_logits'][l].astype(F32), WINDOW).reshape(N_KV, 1, Q_PER_KV * WINDOW)
        u, q, k2, vt = _proj(x, eg, eb, w_ext, wvt, first)
        y = _f1(u, cs, a1)
        fo = _f2(y, a2, params['w_fourier'][l].astype(BF16))
        ao = _attn(q, k2, vt, bias, sink_row)
        x = _post(x, fo, ao, eg, eb,
                  params['w_out'][l].astype(BF16),
                  params['ln_mix_g'][l].reshape(1, -1), params['ln_mix_b'][l].reshape(1, -1),
                  params['w_ff1'][l].astype(BF16), params['w_ff2'][l].astype(BF16),
                  params['ln_ffn_g'][l].reshape(1, -1), params['ln_ffn_b'][l].reshape(1, -1),
                  first, alpha)
    return x


def kernel(x_prompt, x_sample, ln_emb_g, ln_emb_b, w_in, w_fourier, sink_logits, w_out,
           ln_mix_g, ln_mix_b, w_ff1, w_ff2, ln_ffn_g, ln_ffn_b):
    params = dict(ln_emb_g=ln_emb_g, ln_emb_b=ln_emb_b, w_in=w_in, w_fourier=w_fourier,
                  sink_logits=sink_logits, w_out=w_out, ln_mix_g=ln_mix_g, ln_mix_b=ln_mix_b,
                  w_ff1=w_ff1, w_ff2=w_ff2, ln_ffn_g=ln_ffn_g, ln_ffn_b=ln_ffn_b)
    cs = _channel_dft()
    bias = _attn_bias()
    outs = []
    for x in (x_prompt, x_sample):
        a1, a2 = _stage_matrices(x.shape[1])
        outs.append(_trunk(x, (cs, a1, a2, bias), params))
    return tuple(outs)
```

```python
import functools
import math

import numpy as np
import jax
import jax.numpy as jnp
from jax import lax
from jax.experimental import pallas as pl
from jax.experimental.pallas import tpu as pltpu

F32 = jnp.float32
BF16 = jnp.bfloat16

FOURIER_WIDTH = 512
N_GROUPS = 4
GROUP = 128
HEAD_DIM = 64
N_HEADS = 8
N_KV = 2
Q_PER_KV = 4
ATTN_WIDTH = 512
KV_WIDTH = 128
WINDOW = 128
LN_EPS = 1e-5
NEG_BIG = -1e30
LOG2E = math.log2(math.e)
SUBLANES = 8
VMEM_LIMIT = 56 * 1024 * 1024

ROW_TILE = 512
Q_TILE = 512
FF_CHUNK = 1024
POST_SPLIT = 1
ATTN_ITEM_HEADS = 2
ATTN_LOOKAHEAD = 3


def _split(seq):
    lg = int(round(math.log2(seq)))
    assert 2 ** lg == seq
    s1 = 2 ** (lg // 2)
    s2 = seq // s1
    assert s1 % SUBLANES == 0 and s2 % SUBLANES == 0
    return s1, s2


def _const_spec(shape):
    n = len(shape)
    return pl.BlockSpec(shape, lambda *_: (0,) * n, pipeline_mode=pl.Buffered(1))


def _layer_norm(x, g, b):
    mu = jnp.mean(x, axis=-1, keepdims=True)
    xc = x - mu
    var = jnp.mean(xc * xc, axis=-1, keepdims=True)
    return xc * lax.rsqrt(var + LN_EPS) * g + b


def _fold_rows(x, op):
    r = x.shape[0]
    while r % (2 * SUBLANES) == 0:
        x = op(x[:r // 2], x[r // 2:])
        r //= 2
    return x


def _channel_dft():
    k = np.arange(GROUP)
    ang = 2.0 * np.pi * np.outer(k, k) / GROUP
    cs = np.concatenate([np.cos(ang), -np.sin(ang)], axis=1) / np.sqrt(GROUP)
    return jnp.asarray(cs, F32).astype(BF16)


def _stage_matrices(seq):
    s1, s2 = _split(seq)
    i32 = jnp.int32
    sub = jnp.arange(SUBLANES, dtype=i32)

    l1 = 2 * s1 * SUBLANES
    col = jnp.arange(l1, dtype=i32)
    part = col // (s1 * SUBLANES)
    n1c = (col // SUBLANES) % s1
    cc = col % SUBLANES
    k1 = jnp.arange(s1, dtype=i32)
    ang = ((k1[:, None] * n1c[None, :]) % s1).astype(F32) * (2.0 * math.pi / s1)
    c = jnp.cos(ang) * (1.0 / math.sqrt(s1))
    s = jnp.sin(ang) * (1.0 / math.sqrt(s1))
    blk = jnp.stack([jnp.where(part == 0, c, s), jnp.where(part == 0, -s, c)])
    mask = (cc[None, :] == sub[:, None]).astype(F32)
    a1 = (blk[:, None, :, :] * mask[None, :, None, :]).reshape(2 * SUBLANES * s1, l1).astype(BF16)

    l2 = 2 * s2 * SUBLANES
    col = jnp.arange(l2, dtype=i32)
    part = col // (s2 * SUBLANES)
    n2c = (col // SUBLANES) % s2
    cc = col % SUBLANES
    k2 = jnp.arange(s2, dtype=i32)
    anga = ((k2[:, None] * n2c[None, :]) % s2).astype(F32) * (2.0 * math.pi / s2)
    ca = jnp.cos(anga) * (1.0 / math.sqrt(s2))
    sa = jnp.sin(anga) * (1.0 / math.sqrt(s2))
    pm = jnp.where(part == 0, ca, sa)
    qm = jnp.where(part == 0, -sa, ca)
    angb = ((k1[:, None] * n2c[None, :]) % seq).astype(F32) * (2.0 * math.pi / seq)
    keep = cc[None, :] == (k1 % SUBLANES)[:, None]
    cbm = jnp.where(keep, jnp.cos(angb), 0.0).reshape(s1 // SUBLANES, 1, SUBLANES, l2)
    sbm = jnp.where(keep, jnp.sin(angb), 0.0).reshape(s1 // SUBLANES, 1, SUBLANES, l2)
    a2 = pm[None, :, None, :] * cbm + qm[None, :, None, :] * sbm
    a2 = a2.reshape(s1 // SUBLANES, s2 * SUBLANES, l2).astype(BF16)
    return a1, a2


def _attn_bias():
    q = np.arange(WINDOW)[None, :]
    s = np.arange(3 * WINDOW)[:, None]
    rel = s - WINDOW - q
    slopes = 2.0 ** (-8.0 * (np.arange(N_HEADS) + 1.0) / N_HEADS)
    out = np.empty((N_KV, 3 * WINDOW, Q_PER_KV * WINDOW), np.float32)
    for h in range(N_HEADS):
        b = np.where(np.abs(rel) <= WINDOW, -slopes[h] * np.abs(rel) * LOG2E, NEG_BIG)
        kh, g = divmod(h, Q_PER_KV)
        out[kh, :, g * WINDOW:(g + 1) * WINDOW] = b
    return jnp.asarray(out)


def _proj_body(first, x_ref, eg_ref, eb_ref, w_ref, wvt_ref, u_ref, q_ref, k_ref, vt_ref):
    x = x_ref[...]
    if first:
        x = _layer_norm(x, eg_ref[...], eb_ref[...])
    xb = x.astype(BF16)
    p = jnp.dot(xb, w_ref[...], preferred_element_type=F32)
    c1 = FOURIER_WIDTH
    c2 = c1 + ATTN_WIDTH
    u_ref[...] = p[:, :c1]
    q_ref[...] = (p[:, c1:c2] * (LOG2E / math.sqrt(HEAD_DIM))).astype(BF16)
    k_ref[...] = p[:, c2:].astype(BF16)
    vt = lax.dot_general(wvt_ref[...], xb, (((1,), (1,)), ((), ())), preferred_element_type=F32)
    vt_ref[...] = vt.astype(BF16)


def _proj(x, eg, eb, w_ext, wvt, first):
    bsz, seq, d = x.shape
    tm = min(ROW_TILE, seq)
    wn = w_ext.shape[1]
    row = lambda width: pl.BlockSpec((None, tm, width), lambda b, t: (b, t, 0))
    return pl.pallas_call(
        functools.partial(_proj_body, first),
        grid=(bsz, seq // tm),
        in_specs=[row(d), _const_spec((1, d)), _const_spec((1, d)), _const_spec((d, wn)),
                  _const_spec((KV_WIDTH, d))],
        out_specs=[row(FOURIER_WIDTH), row(ATTN_WIDTH), row(2 * KV_WIDTH),
                   pl.BlockSpec((None, KV_WIDTH, tm), lambda b, t: (b, 0, t))],
        out_shape=[jax.ShapeDtypeStruct((bsz, seq, FOURIER_WIDTH), F32),
                   jax.ShapeDtypeStruct((bsz, seq, ATTN_WIDTH), BF16),
                   jax.ShapeDtypeStruct((bsz, seq, 2 * KV_WIDTH), BF16),
                   jax.ShapeDtypeStruct((bsz, KV_WIDTH, seq), BF16)],
        compiler_params=pltpu.CompilerParams(vmem_limit_bytes=VMEM_LIMIT),
        name="proj",
    )(x, eg, eb, w_ext, wvt)


def _f1_body(u_ref, cs_ref, a1_ref, y_ref):
    s1 = u_ref.shape[0]
    r = s1 * SUBLANES
    u = u_ref[...].reshape(r, FOURIER_WIDTH).astype(BF16)
    zr, zi = [], []
    for g in range(N_GROUPS):
        z = jnp.dot(u[:, g * GROUP:(g + 1) * GROUP], cs_ref[...], preferred_element_type=F32)
        zr.append(z[:, :GROUP])
        zi.append(z[:, GROUP:])
    zs = jnp.concatenate([jnp.concatenate(zr, axis=1), jnp.concatenate(zi, axis=1)], axis=0).astype(BF16)
    y = jnp.dot(a1_ref[...], zs, preferred_element_type=F32)
    y_ref[:, :FOURIER_WIDTH] = y[:r]
    y_ref[:, FOURIER_WIDTH:] = y[r:]


def _f1(u, cs, a1):
    bsz, seq, _ = u.shape
    s1, s2 = _split(seq)
    nj = s2 // SUBLANES
    r = s1 * SUBLANES
    u5 = u.reshape(bsz, s1, nj, SUBLANES, FOURIER_WIDTH)
    return pl.pallas_call(
        _f1_body,
        grid=(bsz, nj),
        in_specs=[pl.BlockSpec((None, s1, None, SUBLANES, FOURIER_WIDTH), lambda b, j: (b, 0, j, 0, 0)),
                  _const_spec(cs.shape), _const_spec(a1.shape)],
        out_specs=pl.BlockSpec((None, r, 2 * FOURIER_WIDTH), lambda b, j: (b, j, 0)),
        out_shape=jax.ShapeDtypeStruct((bsz, seq, 2 * FOURIER_WIDTH), F32),
        compiler_params=pltpu.CompilerParams(vmem_limit_bytes=VMEM_LIMIT),
        name="fourier_stage1",
    )(u5, cs, a1)


def _f2_body(y_ref, a2_ref, wf_ref, o_ref):
    s2 = y_ref.shape[0]
    r = s2 * SUBLANES
    y = y_ref[...].reshape(r, 2 * FOURIER_WIDTH)
    ys = jnp.concatenate([y[:, :FOURIER_WIDTH], y[:, FOURIER_WIDTH:]], axis=0).astype(BF16)
    f = jnp.dot(a2_ref[...], ys, preferred_element_type=F32).astype(BF16)
    outs = [jnp.dot(f[:, g * GROUP:(g + 1) * GROUP], wf_ref[g], preferred_element_type=F32)
            for g in range(N_GROUPS)]
    o_ref[...] = jnp.concatenate(outs, axis=1).reshape(s2, SUBLANES, FOURIER_WIDTH)


def _f2(y, a2, wf):
    bsz, seq, _ = y.shape
    s1, s2 = _split(seq)
    ni = s1 // SUBLANES
    r = s2 * SUBLANES
    y5 = y.reshape(bsz, s2, ni, SUBLANES, 2 * FOURIER_WIDTH)
    out = pl.pallas_call(
        _f2_body,
        grid=(ni, bsz),
        in_specs=[pl.BlockSpec((None, s2, None, SUBLANES, 2 * FOURIER_WIDTH), lambda i, b: (b, 0, i, 0, 0)),
                  pl.BlockSpec((None, r, 2 * r), lambda i, b: (i, 0, 0)),
                  _const_spec(wf.shape)],
        out_specs=pl.BlockSpec((None, s2, None, SUBLANES, FOURIER_WIDTH), lambda i, b: (b, 0, i, 0, 0)),
        out_shape=jax.ShapeDtypeStruct((bsz, s2, ni, SUBLANES, FOURIER_WIDTH), F32),
        compiler_params=pltpu.CompilerParams(vmem_limit_bytes=VMEM_LIMIT),
        name="fourier_stage2",
    )(y5, a2, wf)
    return out.reshape(bsz, seq, FOURIER_WIDTH)


def _attn_body(nsub, q_ref, kp_ref, kc_ref, kn_ref, vp_ref, vc_ref, vn_ref, bias_ref, sink_ref, o_ref):
    t = pl.program_id(1)
    last = pl.num_programs(1) - 1
    kcat = jnp.concatenate([kp_ref[...], kc_ref[...], kn_ref[...]], axis=0)
    vtcat = jnp.concatenate([vp_ref[...], vc_ref[...], vn_ref[...]], axis=1)
    neg_first = jnp.where(t == 0, NEG_BIG, 0.0).astype(F32)
    neg_last = jnp.where(t == last, NEG_BIG, 0.0).astype(F32)
    lane = lax.broadcasted_iota(jnp.int32, (WINDOW, 2 * HEAD_DIM), 1)
    m_left = (lane < HEAD_DIM).astype(BF16)
    m_right = 1 - m_left
    pairs_per_item = ATTN_ITEM_HEADS // 2
    items = [(s, kh, p0) for s in range(nsub) for kh in range(N_KV)
             for p0 in range(0, Q_PER_KV // 2, pairs_per_item)]
    width = ATTN_ITEM_HEADS * WINDOW

    def scores(s, kh, p0):
        r0 = s * WINDOW
        parts = []
        for pair in range(p0, p0 + pairs_per_item):
            c0 = (kh * 2 + pair) * GROUP
            qa = q_ref[r0:r0 + WINDOW, c0:c0 + GROUP]
            parts += [qa * m_left, qa * m_right]
        st = jnp.concatenate(parts, axis=0)
        ks = kcat[r0:r0 + 3 * WINDOW, kh * GROUP:(kh + 1) * GROUP]
        sc = lax.dot_general(ks, st, (((1,), (1,)), ((), ())), preferred_element_type=F32)
        sc = sc + bias_ref[kh, :, 2 * p0 * WINDOW:2 * p0 * WINDOW + width]
        if s == 0:
            sc = jnp.concatenate([sc[:WINDOW] + neg_first, sc[WINDOW:]], axis=0)
        if s == nsub - 1:
            sc = jnp.concatenate([sc[:2 * WINDOW], sc[2 * WINDOW:] + neg_last], axis=0)
        return sc

    def softmax(s, kh, p0, sc):
        sink = sink_ref[kh, :, 2 * p0 * WINDOW:2 * p0 * WINDOW + width]
        m = jnp.maximum(jnp.max(_fold_rows(sc, jnp.maximum), axis=0, keepdims=True), sink)
        p = jnp.exp2(sc - m)
        denom = jnp.sum(_fold_rows(p, jnp.add), axis=0, keepdims=True) + jnp.exp2(sink - m)
        return p.astype(BF16), 1.0 / denom

    def weighted_values(s, kh, p0, pn_inv):
        pn, inv = pn_inv
        r0 = s * WINDOW
        vts = vtcat[kh * HEAD_DIM:(kh + 1) * HEAD_DIM, r0:r0 + 3 * WINDOW]
        ot = jnp.dot(vts, pn, preferred_element_type=F32) * inv
        for j in range(pairs_per_item):
            c0 = (kh * 2 + p0 + j) * GROUP
            two = jnp.concatenate([ot[:, 2 * j * WINDOW:(2 * j + 1) * WINDOW],
                                   ot[:, (2 * j + 1) * WINDOW:(2 * j + 2) * WINDOW]], axis=0)
            o_ref[r0:r0 + WINDOW, c0:c0 + GROUP] = two.T

    n = len(items)
    sc = {i: scores(*items[i]) for i in range(min(ATTN_LOOKAHEAD, n))}
    pn_prev = None
    for i in range(n):
        if i + ATTN_LOOKAHEAD < n:
            sc[i + ATTN_LOOKAHEAD] = scores(*items[i + ATTN_LOOKAHEAD])
        pn_cur = softmax(*items[i], sc.pop(i))
        if i >= 1:
            weighted_values(*items[i - 1], pn_prev)
        pn_prev = pn_cur
    weighted_values(*items[n - 1], pn_prev)


def _attn(q, k2, vt, bias, sink_row):
    bsz, seq, _ = q.shape
    tq = min(Q_TILE, seq)
    nsub = tq // WINDOW
    nblk = seq // WINDOW
    kvw = 2 * KV_WIDTH
    kprev = pl.BlockSpec((None, WINDOW, kvw), lambda b, t: (b, jnp.maximum(t * nsub - 1, 0), 0))
    kcur = pl.BlockSpec((None, tq, kvw), lambda b, t: (b, t, 0))
    knext = pl.BlockSpec((None, WINDOW, kvw), lambda b, t: (b, jnp.minimum((t + 1) * nsub, nblk - 1), 0))
    vprev = pl.BlockSpec((None, KV_WIDTH, WINDOW), lambda b, t: (b, 0, jnp.maximum(t * nsub - 1, 0)))
    vcur = pl.BlockSpec((None, KV_WIDTH, tq), lambda b, t: (b, 0, t))
    vnext = pl.BlockSpec((None, KV_WIDTH, WINDOW), lambda b, t: (b, 0, jnp.minimum((t + 1) * nsub, nblk - 1)))
    return pl.pallas_call(
        functools.partial(_attn_body, nsub),
        grid=(bsz, seq // tq),
        in_specs=[pl.BlockSpec((None, tq, ATTN_WIDTH), lambda b, t: (b, t, 0)),
                  kprev, kcur, knext, vprev, vcur, vnext,
                  _const_spec(bias.shape), _const_spec(sink_row.shape)],
        out_specs=pl.BlockSpec((None, tq, ATTN_WIDTH), lambda b, t: (b, t, 0)),
        out_shape=jax.ShapeDtypeStruct((bsz, seq, ATTN_WIDTH), F32),
        compiler_params=pltpu.CompilerParams(vmem_limit_bytes=VMEM_LIMIT),
        name="window_attention",
    )(q, k2, k2, k2, vt, vt, vt, bias, sink_row)


def _post_body(first, alpha, x_ref, fo_ref, ao_ref, eg_ref, eb_ref, wo_ref, g1_ref, b1_ref,
               w1_ref, w2_ref, g2_ref, b2_ref, o_ref):
    dff = w1_ref.shape[1]
    rows = x_ref.shape[0] // POST_SPLIT
    for part in range(POST_SPLIT):
        rs = slice(part * rows, (part + 1) * rows)
        x = x_ref[rs, :]
        if first:
            x = _layer_norm(x, eg_ref[...], eb_ref[...])
        heads = jnp.concatenate([fo_ref[rs, :], ao_ref[rs, :]], axis=1).astype(BF16)
        mix = jnp.dot(heads, wo_ref[...], preferred_element_type=F32)
        x1 = _layer_norm(alpha * x + mix, g1_ref[...], b1_ref[...])
        x1b = x1.astype(BF16)
        z = None
        for c in range(dff // FF_CHUNK):
            h = jnp.dot(x1b, w1_ref[:, c * FF_CHUNK:(c + 1) * FF_CHUNK], preferred_element_type=F32)
            h = jnp.square(jnp.maximum(h, 0.0)).astype(BF16)
            zc = jnp.dot(h, w2_ref[c * FF_CHUNK:(c + 1) * FF_CHUNK, :], preferred_element_type=F32)
            z = zc if z is None else z + zc
        o_ref[rs, :] = _layer_norm(alpha * x1 + z, g2_ref[...], b2_ref[...])


def _post(x, fo, ao, eg, eb, wo, g1, b1, w1, w2, g2, b2, first, alpha):
    bsz, seq, d = x.shape
    tm = min(ROW_TILE, seq)
    dff = w1.shape[1]
    row = lambda width: pl.BlockSpec((None, tm, width), lambda b, t: (b, t, 0))
    vec = _const_spec((1, d))
    return pl.pallas_call(
        functools.partial(_post_body, first, alpha),
        grid=(bsz, seq // tm),
        in_specs=[row(d), row(FOURIER_WIDTH), row(ATTN_WIDTH), vec, vec,
                  _const_spec((d, d)), vec, vec, _const_spec((d, dff)), _const_spec((dff, d)), vec, vec],
        out_specs=row(d),
        out_shape=jax.ShapeDtypeStruct((bsz, seq, d), F32),
        compiler_params=pltpu.CompilerParams(vmem_limit_bytes=VMEM_LIMIT),
        name="mix_mlp",
    )(x, fo, ao, eg, eb, wo, g1, b1, w1, w2, g2, b2)


def _trunk(x, consts, params):
    cs, a1, a2, bias = consts
    depth = params['w_in'].shape[0]
    alpha = (2.0 * depth) ** 0.25
    eg = params['ln_emb_g'].reshape(1, -1)
    eb = params['ln_emb_b'].reshape(1, -1)
    for l in range(depth):
        first = l == 0
        w_in = params['w_in'][l]
        c2 = FOURIER_WIDTH + ATTN_WIDTH
        kcols = [w_in[:, c2 + h * HEAD_DIM:c2 + (h + 1) * HEAD_DIM] for h in range(N_KV)]
        w_ext = jnp.concatenate([w_in[:, :c2]] + [kcols[0]] * 2 + [kcols[1]] * 2, axis=1).astype(BF16)
        wvt = w_in[:, c2 + KV_WIDTH:].T.astype(BF16)
        sink_row = jnp.repeat(params['sink_logits'][l].astype(F32) * LOG2E, WINDOW)
        sink_row = sink_row.reshape(N_KV, 1, Q_PER_KV * WINDOW)
        u, q, k2, vt = _proj(x, eg, eb, w_ext, wvt, first)
        y = _f1(u, cs, a1)
        fo = _f2(y, a2, params['w_fourier'][l].astype(BF16))
        ao = _attn(q, k2, vt, bias, sink_row)
        x = _post(x, fo, ao, eg, eb,
                  params['w_out'][l].astype(BF16),
                  params['ln_mix_g'][l].reshape(1, -1), params['ln_mix_b'][l].reshape(1, -1),
                  params['w_ff1'][l].astype(BF16), params['w_ff2'][l].astype(BF16),
                  params['ln_ffn_g'][l].reshape(1, -1), params['ln_ffn_b'][l].reshape(1, -1),
                  first, alpha)
    return x


def kernel(x_prompt, x_sample, ln_emb_g, ln_emb_b, w_in, w_fourier, sink_logits, w_out,
           ln_mix_g, ln_mix_b, w_ff1, w_ff2, ln_ffn_g, ln_ffn_b):
    params = dict(ln_emb_g=ln_emb_g, ln_emb_b=ln_emb_b, w_in=w_in, w_fourier=w_fourier,
                  sink_logits=sink_logits, w_out=w_out, ln_mix_g=ln_mix_g, ln_mix_b=ln_mix_b,
                  w_ff1=w_ff1, w_ff2=w_ff2, ln_ffn_g=ln_ffn_g, ln_ffn_b=ln_ffn_b)
    cs = _channel_dft()
    bias = _attn_bias()
    outs = []
    for x in (x_prompt, x_sample):
        a1, a2 = _stage_matrices(x.shape[1])
        outs.append(_trunk(x, (cs, a1, a2, bias), params))
    return tuple(outs)
```

```python
import functools
import math

import numpy as np
import jax
import jax.numpy as jnp
from jax import lax
from jax.experimental import pallas as pl
from jax.experimental.pallas import tpu as pltpu

F32 = jnp.float32
BF16 = jnp.bfloat16

FOURIER_WIDTH = 512
N_GROUPS = 4
GROUP = 128
HEAD_DIM = 64
N_HEADS = 8
N_KV = 2
Q_PER_KV = 4
ATTN_WIDTH = 512
KV_WIDTH = 128
WINDOW = 128
LN_EPS = 1e-5
NEG_BIG = -1e30
LOG2E = math.log2(math.e)
SUBLANES = 8
VMEM_LIMIT = 56 * 1024 * 1024

ROW_TILE = 512
Q_TILE = 512
FF_CHUNK = 1024
ATTN_ITEM_HEADS = 2
ATTN_LOOKAHEAD = 3


DFT1 = 256
F1_LANES = 2048
F2_ROWS = 512


def _split(seq):
    s2 = seq // DFT1
    assert s2 * DFT1 == seq and (SUBLANES * s2) % 16 == 0
    return s2


def _const_spec(shape):
    n = len(shape)
    return pl.BlockSpec(shape, lambda *_: (0,) * n, pipeline_mode=pl.Buffered(1))


def _layer_norm(x, g, b):
    mu = jnp.mean(x, axis=-1, keepdims=True)
    xc = x - mu
    var = jnp.mean(xc * xc, axis=-1, keepdims=True)
    return xc * lax.rsqrt(var + LN_EPS) * g + b


def _fold_rows(x, op):
    r = x.shape[0]
    while r % (2 * SUBLANES) == 0:
        x = op(x[:r // 2], x[r // 2:])
        r //= 2
    return x


def _channel_dft():
    k = np.arange(GROUP)
    ang = 2.0 * np.pi * np.outer(k, k) / GROUP
    cs = np.concatenate([np.cos(ang), np.sin(ang)], axis=0) / np.sqrt(GROUP)
    return jnp.asarray(cs, F32).astype(BF16)


def _stage1_matrix():
    k = np.arange(DFT1)
    ang = 2.0 * np.pi * np.outer(k, k) / DFT1
    a = np.concatenate([np.cos(ang), -np.sin(ang)], axis=0) / np.sqrt(DFT1)
    return jnp.asarray(a, F32).astype(BF16)


def _stage2_matrices(seq):
    s2 = _split(seq)
    r = SUBLANES * s2
    i32 = jnp.int32
    col = jnp.arange(2 * r, dtype=i32)
    part = col // r
    cb = (col % r) // s2
    n2c = col % s2
    k2 = jnp.arange(s2, dtype=i32)
    anga = ((k2[:, None] * n2c[None, :]) % s2).astype(F32) * (2.0 * math.pi / s2)
    ca = jnp.cos(anga) * (1.0 / math.sqrt(s2))
    sa = jnp.sin(anga) * (1.0 / math.sqrt(s2))
    pm = jnp.stack([jnp.where(part == 0, ca, sa), jnp.where(part == 0, -sa, ca)])
    qm = jnp.stack([jnp.where(part == 0, -sa, ca), jnp.where(part == 0, -ca, -sa)])
    k1 = jnp.arange(DFT1, dtype=i32)
    angb = ((k1[:, None] * n2c[None, :]) % seq).astype(F32) * (2.0 * math.pi / seq)
    keep = cb[None, :] == (k1 % SUBLANES)[:, None]
    ni = DFT1 // SUBLANES
    cbm = jnp.where(keep, jnp.cos(angb), 0.0).reshape(ni, 1, 1, SUBLANES, 2 * r)
    sbm = jnp.where(keep, jnp.sin(angb), 0.0).reshape(ni, 1, 1, SUBLANES, 2 * r)
    a2 = pm[None, :, :, None, :] * cbm + qm[None, :, :, None, :] * sbm
    return a2.reshape(ni, 2 * r, 2 * r).astype(BF16)


def _attn_bias():
    q = np.arange(WINDOW)[None, :]
    s = np.arange(3 * WINDOW)[:, None]
    rel = s - WINDOW - q
    slopes = 2.0 ** (-8.0 * (np.arange(N_HEADS) + 1.0) / N_HEADS)
    out = np.empty((N_KV, 3 * WINDOW, Q_PER_KV * WINDOW), np.float32)
    for h in range(N_HEADS):
        b = np.where(np.abs(rel) <= WINDOW, -slopes[h] * np.abs(rel) * LOG2E, NEG_BIG)
        kh, g = divmod(h, Q_PER_KV)
        out[kh, :, g * WINDOW:(g + 1) * WINDOW] = b
    return jnp.asarray(out)


def _proj_body(first, x_ref, eg_ref, eb_ref, w_ref, u_ref, q_ref, k_ref, vt_ref):
    x = x_ref[...]
    if first:
        x = _layer_norm(x, eg_ref[...], eb_ref[...])
    p = jnp.dot(x.astype(BF16), w_ref[...], preferred_element_type=F32)
    c1 = FOURIER_WIDTH
    c2 = c1 + ATTN_WIDTH
    c3 = c2 + 2 * KV_WIDTH
    u_ref[...] = p[:, :c1].astype(BF16)
    q_ref[...] = (p[:, c1:c2] * (LOG2E / math.sqrt(HEAD_DIM))).astype(BF16)
    k_ref[...] = p[:, c2:c3].astype(BF16)
    vt_ref[...] = p[:, c3:].T.astype(BF16)


def _proj(x, eg, eb, w_ext, first):
    bsz, seq, d = x.shape
    tm = min(ROW_TILE, seq)
    wn = w_ext.shape[1]
    row = lambda width: pl.BlockSpec((None, tm, width), lambda b, t: (b, t, 0))
    return pl.pallas_call(
        functools.partial(_proj_body, first),
        grid=(bsz, seq // tm),
        in_specs=[row(d), _const_spec((1, d)), _const_spec((1, d)), _const_spec((d, wn))],
        out_specs=[row(FOURIER_WIDTH), row(ATTN_WIDTH), row(2 * KV_WIDTH),
                   pl.BlockSpec((None, KV_WIDTH, tm), lambda b, t: (b, 0, t))],
        out_shape=[jax.ShapeDtypeStruct((bsz, seq, FOURIER_WIDTH), BF16),
                   jax.ShapeDtypeStruct((bsz, seq, ATTN_WIDTH), BF16),
                   jax.ShapeDtypeStruct((bsz, seq, 2 * KV_WIDTH), BF16),
                   jax.ShapeDtypeStruct((bsz, KV_WIDTH, seq), BF16)],
        compiler_params=pltpu.CompilerParams(vmem_limit_bytes=VMEM_LIMIT),
        name="proj",
    )(x, eg, eb, w_ext)


def _f1_body(u_ref, a1_ref, y_ref):
    y = jnp.dot(a1_ref[...], u_ref[...], preferred_element_type=F32).astype(BF16)
    y_ref[0] = y[:DFT1]
    y_ref[1] = y[DFT1:]


def _f1(u, a1):
    bsz, seq, width = u.shape
    s2 = _split(seq)
    lanes = s2 * width
    lb = min(F1_LANES, lanes)
    u2 = u.reshape(bsz, DFT1, lanes)
    y = pl.pallas_call(
        _f1_body,
        grid=(bsz, lanes // lb),
        in_specs=[pl.BlockSpec((None, DFT1, lb), lambda b, j: (b, 0, j)), _const_spec(a1.shape)],
        out_specs=pl.BlockSpec((None, 2, DFT1, lb), lambda b, j: (b, 0, 0, j)),
        out_shape=jax.ShapeDtypeStruct((bsz, 2, DFT1, lanes), BF16),
        compiler_params=pltpu.CompilerParams(vmem_limit_bytes=VMEM_LIMIT),
        name="fourier_stage1",
    )(u2, a1)
    return y.reshape(bsz, 2, DFT1 * s2, width)


def _f2_body(nitem, y_ref, a2_ref, cm_ref, wf_ref, o_ref):
    r = y_ref.shape[1] // nitem
    s2 = r // SUBLANES
    pr, pi = [], []
    for c in range(nitem):
        ys = jnp.concatenate([y_ref[0, c * r:(c + 1) * r, :], y_ref[1, c * r:(c + 1) * r, :]], axis=0)
        p = jnp.dot(a2_ref[c], ys, preferred_element_type=F32)
        pr.append(p[:r])
        pi.append(p[r:])
    pr = jnp.concatenate(pr, axis=0).astype(BF16)
    pi = jnp.concatenate(pi, axis=0).astype(BF16)
    rows = nitem * r
    both = jnp.concatenate(
        [jnp.concatenate([pr[:, g * GROUP:(g + 1) * GROUP], pi[:, g * GROUP:(g + 1) * GROUP]], axis=1)
         for g in range(N_GROUPS)], axis=0)
    f = jnp.dot(both, cm_ref[...], preferred_element_type=F32).astype(BF16)
    outs = [jnp.dot(f[g * rows:(g + 1) * rows], wf_ref[g], preferred_element_type=F32)
            for g in range(N_GROUPS)]
    out = jnp.concatenate(outs, axis=1)
    for c in range(nitem):
        o_ref[:, c * SUBLANES:(c + 1) * SUBLANES, :] = out[c * r:(c + 1) * r].reshape(s2, SUBLANES, FOURIER_WIDTH)


def _f2(y, a2, cm, wf):
    bsz, _, seq, width = y.shape
    s2 = _split(seq)
    r = SUBLANES * s2
    nitem = max(1, F2_ROWS // r)
    nm = DFT1 // (SUBLANES * nitem)
    out = pl.pallas_call(
        functools.partial(_f2_body, nitem),
        grid=(nm, bsz),
        in_specs=[pl.BlockSpec((None, 2, nitem * r, width), lambda m, b: (b, 0, m, 0)),
                  pl.BlockSpec((nitem, 2 * r, 2 * r), lambda m, b: (m, 0, 0)),
                  _const_spec(cm.shape), _const_spec(wf.shape)],
        out_specs=pl.BlockSpec((None, s2, None, nitem * SUBLANES, width), lambda m, b: (b, 0, m, 0, 0)),
        out_shape=jax.ShapeDtypeStruct((bsz, s2, nm, nitem * SUBLANES, width), F32),
        compiler_params=pltpu.CompilerParams(vmem_limit_bytes=VMEM_LIMIT),
        name="fourier_stage2",
    )(y, a2, cm, wf)
    return out.reshape(bsz, seq, width)


def _attn_body(nsub, q_ref, kp_ref, kc_ref, kn_ref, vp_ref, vc_ref, vn_ref, bias_ref, sink_ref, o_ref):
    t = pl.program_id(1)
    last = pl.num_programs(1) - 1
    kcat = jnp.concatenate([kp_ref[...], kc_ref[...], kn_ref[...]], axis=0)
    vtcat = jnp.concatenate([vp_ref[...], vc_ref[...], vn_ref[...]], axis=1)
    neg_first = jnp.where(t == 0, NEG_BIG, 0.0).astype(F32)
    neg_last = jnp.where(t == last, NEG_BIG, 0.0).astype(F32)
    lane = lax.broadcasted_iota(jnp.int32, (WINDOW, 2 * HEAD_DIM), 1)
    m_left = (lane < HEAD_DIM).astype(BF16)
    m_right = 1 - m_left
    pairs_per_item = ATTN_ITEM_HEADS // 2
    items = [(s, kh, p0) for s in range(nsub) for kh in range(N_KV)
             for p0 in range(0, Q_PER_KV // 2, pairs_per_item)]
    width = ATTN_ITEM_HEADS * WINDOW

    def scores(s, kh, p0):
        r0 = s * WINDOW
        parts = []
        for pair in range(p0, p0 + pairs_per_item):
            c0 = (kh * 2 + pair) * GROUP
            qa = q_ref[r0:r0 + WINDOW, c0:c0 + GROUP]
            parts += [qa * m_left, qa * m_right]
        st = jnp.concatenate(parts, axis=0)
        ks = kcat[r0:r0 + 3 * WINDOW, kh * GROUP:(kh + 1) * GROUP]
        sc = lax.dot_general(ks, st, (((1,), (1,)), ((), ())), preferred_element_type=F32)
        sc = sc + bias_ref[kh, :, 2 * p0 * WINDOW:2 * p0 * WINDOW + width]
        if s == 0:
            sc = jnp.concatenate([sc[:WINDOW] + neg_first, sc[WINDOW:]], axis=0)
        if s == nsub - 1:
            sc = jnp.concatenate([sc[:2 * WINDOW], sc[2 * WINDOW:] + neg_last], axis=0)
        return sc

    def softmax(s, kh, p0, sc):
        sink = sink_ref[kh, :, 2 * p0 * WINDOW:2 * p0 * WINDOW + width]
        m = jnp.maximum(jnp.max(_fold_rows(sc, jnp.maximum), axis=0, keepdims=True), sink)
        p = jnp.exp2(sc - m)
        denom = jnp.sum(_fold_rows(p, jnp.add), axis=0, keepdims=True) + jnp.exp2(sink - m)
        return p.astype(BF16), 1.0 / denom

    def weighted_values(s, kh, p0, pn_inv):
        pn, inv = pn_inv
        r0 = s * WINDOW
        vts = vtcat[kh * HEAD_DIM:(kh + 1) * HEAD_DIM, r0:r0 + 3 * WINDOW]
        ot = jnp.dot(vts, pn, preferred_element_type=F32) * inv
        for j in range(pairs_per_item):
            c0 = (kh * 2 + p0 + j) * GROUP
            two = jnp.concatenate([ot[:, 2 * j * WINDOW:(2 * j + 1) * WINDOW],
                                   ot[:, (2 * j + 1) * WINDOW:(2 * j + 2) * WINDOW]], axis=0)
            o_ref[r0:r0 + WINDOW, c0:c0 + GROUP] = two.T

    n = len(items)
    sc = {i: scores(*items[i]) for i in range(min(ATTN_LOOKAHEAD, n))}
    pn_prev = None
    for i in range(n):
        if i + ATTN_LOOKAHEAD < n:
            sc[i + ATTN_LOOKAHEAD] = scores(*items[i + ATTN_LOOKAHEAD])
        pn_cur = softmax(*items[i], sc.pop(i))
        if i >= 1:
            weighted_values(*items[i - 1], pn_prev)
        pn_prev = pn_cur
    weighted_values(*items[n - 1], pn_prev)


def _attn(q, k2, vt, bias, sink_row):
    bsz, seq, _ = q.shape
    tq = min(Q_TILE, seq)
    nsub = tq // WINDOW
    nblk = seq // WINDOW
    kvw = 2 * KV_WIDTH
    kprev = pl.BlockSpec((None, WINDOW, kvw), lambda b, t: (b, jnp.maximum(t * nsub - 1, 0), 0))
    kcur = pl.BlockSpec((None, tq, kvw), lambda b, t: (b, t, 0))
    knext = pl.BlockSpec((None, WINDOW, kvw), lambda b, t: (b, jnp.minimum((t + 1) * nsub, nblk - 1), 0))
    vprev = pl.BlockSpec((None, KV_WIDTH, WINDOW), lambda b, t: (b, 0, jnp.maximum(t * nsub - 1, 0)))
    vcur = pl.BlockSpec((None, KV_WIDTH, tq), lambda b, t: (b, 0, t))
    vnext = pl.BlockSpec((None, KV_WIDTH, WINDOW), lambda b, t: (b, 0, jnp.minimum((t + 1) * nsub, nblk - 1)))
    return pl.pallas_call(
        functools.partial(_attn_body, nsub),
        grid=(bsz, seq // tq),
        in_specs=[pl.BlockSpec((None, tq, ATTN_WIDTH), lambda b, t: (b, t, 0)),
                  kprev, kcur, knext, vprev, vcur, vnext,
                  _const_spec(bias.shape), _const_spec(sink_row.shape)],
        out_specs=pl.BlockSpec((None, tq, ATTN_WIDTH), lambda b, t: (b, t, 0)),
        out_shape=jax.ShapeDtypeStruct((bsz, seq, ATTN_WIDTH), F32),
        compiler_params=pltpu.CompilerParams(vmem_limit_bytes=VMEM_LIMIT),
        name="window_attention",
    )(q, k2, k2, k2, vt, vt, vt, bias, sink_row)


def _post_body(first, alpha, x_ref, fo_ref, ao_ref, eg_ref, eb_ref, wo_ref, g1_ref, b1_ref,
               w1_ref, w2_ref, g2_ref, b2_ref, o_ref):
    dff = w1_ref.shape[1]
    x = x_ref[...]
    if first:
        x = _layer_norm(x, eg_ref[...], eb_ref[...])
    heads = jnp.concatenate([fo_ref[...], ao_ref[...]], axis=1).astype(BF16)
    mix = jnp.dot(heads, wo_ref[...], preferred_element_type=F32)
    x1 = _layer_norm(alpha * x + mix, g1_ref[...], b1_ref[...])
    x1b = x1.astype(BF16)
    z = None
    for c in range(dff // FF_CHUNK):
        h = jnp.dot(x1b, w1_ref[:, c * FF_CHUNK:(c + 1) * FF_CHUNK], preferred_element_type=F32)
        h = jnp.square(jnp.maximum(h, 0.0)).astype(BF16)
        zc = jnp.dot(h, w2_ref[c * FF_CHUNK:(c + 1) * FF_CHUNK, :], preferred_element_type=F32)
        z = zc if z is None else z + zc
    o_ref[...] = _layer_norm(alpha * x1 + z, g2_ref[...], b2_ref[...])


def _post(x, fo, ao, eg, eb, wo, g1, b1, w1, w2, g2, b2, first, alpha):
    bsz, seq, d = x.shape
    tm = min(ROW_TILE, seq)
    dff = w1.shape[1]
    row = lambda width: pl.BlockSpec((None, tm, width), lambda b, t: (b, t, 0))
    vec = _const_spec((1, d))
    return pl.pallas_call(
        functools.partial(_post_body, first, alpha),
        grid=(bsz, seq // tm),
        in_specs=[row(d), row(FOURIER_WIDTH), row(ATTN_WIDTH), vec, vec,
                  _const_spec((d, d)), vec, vec, _const_spec((d, dff)), _const_spec((dff, d)), vec, vec],
        out_specs=row(d),
        out_shape=jax.ShapeDtypeStruct((bsz, seq, d), F32),
        compiler_params=pltpu.CompilerParams(vmem_limit_bytes=VMEM_LIMIT),
        name="mix_mlp",
    )(x, fo, ao, eg, eb, wo, g1, b1, w1, w2, g2, b2)


def _trunk(x, consts, params):
    cm, a1, a2, bias = consts
    depth = params['w_in'].shape[0]
    alpha = (2.0 * depth) ** 0.25
    eg = params['ln_emb_g'].reshape(1, -1)
    eb = params['ln_emb_b'].reshape(1, -1)
    for l in range(depth):
        first = l == 0
        w_in = params['w_in'][l]
        c2 = FOURIER_WIDTH + ATTN_WIDTH
        kcols = [w_in[:, c2 + h * HEAD_DIM:c2 + (h + 1) * HEAD_DIM] for h in range(N_KV)]
        w_ext = jnp.concatenate([w_in[:, :c2]] + [kcols[0]] * 2 + [kcols[1]] * 2
                                + [w_in[:, c2 + KV_WIDTH:]], axis=1).astype(BF16)
        sink_row = jnp.repeat(params['sink_logits'][l].astype(F32) * LOG2E, WINDOW)
        sink_row = sink_row.reshape(N_KV, 1, Q_PER_KV * WINDOW)
        u, q, k2, vt = _proj(x, eg, eb, w_ext, first)
        y = _f1(u, a1)
        fo = _f2(y, a2, cm, params['w_fourier'][l].astype(BF16))
        ao = _attn(q, k2, vt, bias, sink_row)
        x = _post(x, fo, ao, eg, eb,
                  params['w_out'][l].astype(BF16),
                  params['ln_mix_g'][l].reshape(1, -1), params['ln_mix_b'][l].reshape(1, -1),
                  params['w_ff1'][l].astype(BF16), params['w_ff2'][l].astype(BF16),
                  params['ln_ffn_g'][l].reshape(1, -1), params['ln_ffn_b'][l].reshape(1, -1),
                  first, alpha)
    return x


def kernel(x_prompt, x_sample, ln_emb_g, ln_emb_b, w_in, w_fourier, sink_logits, w_out,
           ln_mix_g, ln_mix_b, w_ff1, w_ff2, ln_ffn_g, ln_ffn_b):
    params = dict(ln_emb_g=ln_emb_g, ln_emb_b=ln_emb_b, w_in=w_in, w_fourier=w_fourier,
                  sink_logits=sink_logits, w_out=w_out, ln_mix_g=ln_mix_g, ln_mix_b=ln_mix_b,
                  w_ff1=w_ff1, w_ff2=w_ff2, ln_ffn_g=ln_ffn_g, ln_ffn_b=ln_ffn_b)
    cm = _channel_dft()
    a1 = _stage1_matrix()
    bias = _attn_bias()
    outs = []
    for x in (x_prompt, x_sample):
        a2 = _stage2_matrices(x.shape[1])
        outs.append(_trunk(x, (cm, a1, a2, bias), params))
    return tuple(outs)
```

```python
import functools
import math

import numpy as np
import jax
import jax.numpy as jnp
from jax import lax
from jax.experimental import pallas as pl
from jax.experimental.pallas import tpu as pltpu

F32 = jnp.float32
BF16 = jnp.bfloat16

FOURIER_WIDTH = 512
N_GROUPS = 4
GROUP = 128
HEAD_DIM = 64
N_HEADS = 8
N_KV = 2
Q_PER_KV = 4
ATTN_WIDTH = 512
KV_WIDTH = 128
WINDOW = 128
LN_EPS = 1e-5
NEG_BIG = -1e30
LOG2E = math.log2(math.e)
SUBLANES = 8
VMEM_LIMIT = 56 * 1024 * 1024

ROW_TILE = 512
Q_TILE = 512
FF_CHUNK = 1024
ATTN_ITEM_HEADS = 2
ATTN_LOOKAHEAD = 3


DFT1 = 256
F1_LANES = 2048
F2_ROWS = 512


def _split(seq):
    s2 = seq // DFT1
    assert s2 * DFT1 == seq and (SUBLANES * s2) % 16 == 0
    return s2


def _const_spec(shape):
    n = len(shape)
    return pl.BlockSpec(shape, lambda *_: (0,) * n, pipeline_mode=pl.Buffered(1))


def _layer_norm(x, g, b):
    mu = jnp.mean(x, axis=-1, keepdims=True)
    xc = x - mu
    var = jnp.mean(xc * xc, axis=-1, keepdims=True)
    return xc * lax.rsqrt(var + LN_EPS) * g + b


def _fold_rows(x, op):
    r = x.shape[0]
    while r % (2 * SUBLANES) == 0:
        x = op(x[:r // 2], x[r // 2:])
        r //= 2
    return x


def _channel_dft():
    k = np.arange(GROUP)
    ang = 2.0 * np.pi * np.outer(k, k) / GROUP
    cs = np.concatenate([np.cos(ang), np.sin(ang)], axis=0) / np.sqrt(GROUP)
    return jnp.asarray(cs, F32).astype(BF16)


def _stage1_matrix():
    k = np.arange(DFT1)
    ang = 2.0 * np.pi * np.outer(k, k) / DFT1
    a = np.concatenate([np.cos(ang), -np.sin(ang)], axis=0) / np.sqrt(DFT1)
    return jnp.asarray(a, F32).astype(BF16)


def _stage2_matrices(seq):
    s2 = _split(seq)
    r = SUBLANES * s2
    i32 = jnp.int32
    col = jnp.arange(2 * r, dtype=i32)
    part = col // r
    n2c = (col % r) // SUBLANES
    cb = col % SUBLANES
    k2 = jnp.arange(s2, dtype=i32)
    anga = ((k2[:, None] * n2c[None, :]) % s2).astype(F32) * (2.0 * math.pi / s2)
    ca = jnp.cos(anga) * (1.0 / math.sqrt(s2))
    sa = jnp.sin(anga) * (1.0 / math.sqrt(s2))
    pm = jnp.stack([jnp.where(part == 0, ca, sa), jnp.where(part == 0, -sa, ca)])
    qm = jnp.stack([jnp.where(part == 0, -sa, ca), jnp.where(part == 0, -ca, -sa)])
    k1 = jnp.arange(DFT1, dtype=i32)
    angb = ((k1[:, None] * n2c[None, :]) % seq).astype(F32) * (2.0 * math.pi / seq)
    keep = cb[None, :] == (k1 % SUBLANES)[:, None]
    ni = DFT1 // SUBLANES
    cbm = jnp.where(keep, jnp.cos(angb), 0.0).reshape(ni, 1, 1, SUBLANES, 2 * r)
    sbm = jnp.where(keep, jnp.sin(angb), 0.0).reshape(ni, 1, 1, SUBLANES, 2 * r)
    a2 = pm[None, :, :, None, :] * cbm + qm[None, :, :, None, :] * sbm
    return a2.reshape(ni, 2 * r, 2 * r).astype(BF16)


def _attn_bias():
    q = np.arange(WINDOW)[None, :]
    s = np.arange(3 * WINDOW)[:, None]
    rel = s - WINDOW - q
    slopes = 2.0 ** (-8.0 * (np.arange(N_HEADS) + 1.0) / N_HEADS)
    out = np.empty((N_KV, 3 * WINDOW, Q_PER_KV * WINDOW), np.float32)
    for h in range(N_HEADS):
        b = np.where(np.abs(rel) <= WINDOW, -slopes[h] * np.abs(rel) * LOG2E, NEG_BIG)
        kh, g = divmod(h, Q_PER_KV)
        out[kh, :, g * WINDOW:(g + 1) * WINDOW] = b
    return jnp.asarray(out)


def _proj_body(first, s2, x_ref, eg_ref, eb_ref, w_ref, u_ref, q_ref, k_ref, vt_ref, us_scr):
    x = x_ref[...]
    if first:
        x = _layer_norm(x, eg_ref[...], eb_ref[...])
    p = jnp.dot(x.astype(BF16), w_ref[...], preferred_element_type=F32)
    c1 = FOURIER_WIDTH
    c2 = c1 + ATTN_WIDTH
    c3 = c2 + 2 * KV_WIDTH
    q_ref[...] = (p[:, c1:c2] * (LOG2E / math.sqrt(HEAD_DIM))).astype(BF16)
    k_ref[...] = p[:, c2:c3].astype(BF16)
    vt_ref[...] = p[:, c3:].T.astype(BF16)
    nrow = x_ref.shape[0] // s2
    for g in range(N_GROUPS):
        us_scr[g] = p[:, g * GROUP:(g + 1) * GROUP]
    for n2 in range(s2):
        for g in range(N_GROUPS):
            piece = us_scr[g, pl.ds(n2, nrow, stride=s2), :]
            lo = n2 * FOURIER_WIDTH + g * GROUP
            u_ref[:, lo:lo + GROUP] = piece.astype(BF16)


def _proj(x, eg, eb, w_ext, first):
    bsz, seq, d = x.shape
    s2 = _split(seq)
    tm = max(ROW_TILE, 16 * s2)
    wn = w_ext.shape[1]
    row = lambda width: pl.BlockSpec((None, tm, width), lambda b, t: (b, t, 0))
    return pl.pallas_call(
        functools.partial(_proj_body, first, s2),
        grid=(bsz, seq // tm),
        in_specs=[row(d), _const_spec((1, d)), _const_spec((1, d)), _const_spec((d, wn))],
        out_specs=[pl.BlockSpec((None, tm // s2, s2 * FOURIER_WIDTH), lambda b, t: (b, t, 0)),
                   row(ATTN_WIDTH), row(2 * KV_WIDTH),
                   pl.BlockSpec((None, KV_WIDTH, tm), lambda b, t: (b, 0, t))],
        out_shape=[jax.ShapeDtypeStruct((bsz, DFT1, s2 * FOURIER_WIDTH), BF16),
                   jax.ShapeDtypeStruct((bsz, seq, ATTN_WIDTH), BF16),
                   jax.ShapeDtypeStruct((bsz, seq, 2 * KV_WIDTH), BF16),
                   jax.ShapeDtypeStruct((bsz, KV_WIDTH, seq), BF16)],
        scratch_shapes=[pltpu.VMEM((N_GROUPS, tm, GROUP), F32)],
        compiler_params=pltpu.CompilerParams(vmem_limit_bytes=VMEM_LIMIT),
        name="proj",
    )(x, eg, eb, w_ext)


def _f1_body(u_ref, a1_ref, y_ref):
    y = jnp.dot(a1_ref[...], u_ref[...], preferred_element_type=F32)
    y_ref[0] = y[:DFT1]
    y_ref[1] = y[DFT1:]


def _f1(u2, a1):
    bsz, _, lanes = u2.shape
    lb = min(F1_LANES, lanes)
    return pl.pallas_call(
        _f1_body,
        grid=(bsz, lanes // lb),
        in_specs=[pl.BlockSpec((None, DFT1, lb), lambda b, j: (b, 0, j)), _const_spec(a1.shape)],
        out_specs=pl.BlockSpec((None, 2, DFT1, lb), lambda b, j: (b, 0, 0, j)),
        out_shape=jax.ShapeDtypeStruct((bsz, 2, DFT1, lanes), F32),
        compiler_params=pltpu.CompilerParams(vmem_limit_bytes=VMEM_LIMIT),
        name="fourier_stage1",
    )(u2, a1)


def _f2_body(nitem, y_ref, a2_ref, cm_ref, wf_ref, o_ref):
    width = FOURIER_WIDTH
    s2 = y_ref.shape[2] // width
    r = SUBLANES * s2
    pr, pi = [], []
    for c in range(nitem):
        rs = slice(c * SUBLANES, (c + 1) * SUBLANES)
        ys = jnp.concatenate([y_ref[part, rs, n2 * width:(n2 + 1) * width]
                              for part in range(2) for n2 in range(s2)], axis=0).astype(BF16)
        p = jnp.dot(a2_ref[c], ys, preferred_element_type=F32)
        pr.append(p[:r])
        pi.append(p[r:])
    pr = jnp.concatenate(pr, axis=0).astype(BF16)
    pi = jnp.concatenate(pi, axis=0).astype(BF16)
    rows = nitem * r
    both = jnp.concatenate(
        [jnp.concatenate([pr[:, g * GROUP:(g + 1) * GROUP], pi[:, g * GROUP:(g + 1) * GROUP]], axis=1)
         for g in range(N_GROUPS)], axis=0)
    f = jnp.dot(both, cm_ref[...], preferred_element_type=F32).astype(BF16)
    outs = [jnp.dot(f[g * rows:(g + 1) * rows], wf_ref[g], preferred_element_type=F32)
            for g in range(N_GROUPS)]
    out = jnp.concatenate(outs, axis=1)
    for c in range(nitem):
        o_ref[:, c * SUBLANES:(c + 1) * SUBLANES, :] = out[c * r:(c + 1) * r].reshape(s2, SUBLANES, FOURIER_WIDTH)


def _f2(y, a2, cm, wf):
    bsz, _, _, lanes = y.shape
    width = FOURIER_WIDTH
    s2 = lanes // width
    seq = DFT1 * s2
    r = SUBLANES * s2
    nitem = max(1, F2_ROWS // r)
    nm = DFT1 // (SUBLANES * nitem)
    out = pl.pallas_call(
        functools.partial(_f2_body, nitem),
        grid=(nm, bsz),
        in_specs=[pl.BlockSpec((None, 2, nitem * SUBLANES, lanes), lambda m, b: (b, 0, m, 0)),
                  pl.BlockSpec((nitem, 2 * r, 2 * r), lambda m, b: (m, 0, 0)),
                  _const_spec(cm.shape), _const_spec(wf.shape)],
        out_specs=pl.BlockSpec((None, s2, None, nitem * SUBLANES, width), lambda m, b: (b, 0, m, 0, 0)),
        out_shape=jax.ShapeDtypeStruct((bsz, s2, nm, nitem * SUBLANES, width), F32),
        compiler_params=pltpu.CompilerParams(vmem_limit_bytes=VMEM_LIMIT),
        name="fourier_stage2",
    )(y, a2, cm, wf)
    return out.reshape(bsz, seq, width)


def _attn_body(nsub, q_ref, kp_ref, kc_ref, kn_ref, vp_ref, vc_ref, vn_ref, bias_ref, sink_ref, o_ref):
    t = pl.program_id(1)
    last = pl.num_programs(1) - 1
    kcat = jnp.concatenate([kp_ref[...], kc_ref[...], kn_ref[...]], axis=0)
    vtcat = jnp.concatenate([vp_ref[...], vc_ref[...], vn_ref[...]], axis=1)
    neg_first = jnp.where(t == 0, NEG_BIG, 0.0).astype(F32)
    neg_last = jnp.where(t == last, NEG_BIG, 0.0).astype(F32)
    lane = lax.broadcasted_iota(jnp.int32, (WINDOW, 2 * HEAD_DIM), 1)
    m_left = (lane < HEAD_DIM).astype(BF16)
    m_right = 1 - m_left
    pairs_per_item = ATTN_ITEM_HEADS // 2
    items = [(s, kh, p0) for s in range(nsub) for kh in range(N_KV)
             for p0 in range(0, Q_PER_KV // 2, pairs_per_item)]
    width = ATTN_ITEM_HEADS * WINDOW

    def scores(s, kh, p0):
        r0 = s * WINDOW
        parts = []
        for pair in range(p0, p0 + pairs_per_item):
            c0 = (kh * 2 + pair) * GROUP
            qa = q_ref[r0:r0 + WINDOW, c0:c0 + GROUP]
            parts += [qa * m_left, qa * m_right]
        st = jnp.concatenate(parts, axis=0)
        ks = kcat[r0:r0 + 3 * WINDOW, kh * GROUP:(kh + 1) * GROUP]
        sc = lax.dot_general(ks, st, (((1,), (1,)), ((), ())), preferred_element_type=F32)
        sc = sc + bias_ref[kh, :, 2 * p0 * WINDOW:2 * p0 * WINDOW + width]
        if s == 0:
            sc = jnp.concatenate([sc[:WINDOW] + neg_first, sc[WINDOW:]], axis=0)
        if s == nsub - 1:
            sc = jnp.concatenate([sc[:2 * WINDOW], sc[2 * WINDOW:] + neg_last], axis=0)
        return sc

    def softmax(s, kh, p0, sc):
        sink = sink_ref[kh, :, 2 * p0 * WINDOW:2 * p0 * WINDOW + width]
        m = jnp.maximum(jnp.max(_fold_rows(sc, jnp.maximum), axis=0, keepdims=True), sink)
        p = jnp.exp2(sc - m)
        denom = jnp.sum(_fold_rows(p, jnp.add), axis=0, keepdims=True) + jnp.exp2(sink - m)
        return p.astype(BF16), 1.0 / denom

    def weighted_values(s, kh, p0, pn_inv):
        pn, inv = pn_inv
        r0 = s * WINDOW
        vts = vtcat[kh * HEAD_DIM:(kh + 1) * HEAD_DIM, r0:r0 + 3 * WINDOW]
        ot = jnp.dot(vts, pn, preferred_element_type=F32) * inv
        for j in range(pairs_per_item):
            c0 = (kh * 2 + p0 + j) * GROUP
            two = jnp.concatenate([ot[:, 2 * j * WINDOW:(2 * j + 1) * WINDOW],
                                   ot[:, (2 * j + 1) * WINDOW:(2 * j + 2) * WINDOW]], axis=0)
            o_ref[r0:r0 + WINDOW, c0:c0 + GROUP] = two.T

    n = len(items)
    sc = {i: scores(*items[i]) for i in range(min(ATTN_LOOKAHEAD, n))}
    pn_prev = None
    for i in range(n):
        if i + ATTN_LOOKAHEAD < n:
            sc[i + ATTN_LOOKAHEAD] = scores(*items[i + ATTN_LOOKAHEAD])
        pn_cur = softmax(*items[i], sc.pop(i))
        if i >= 1:
            weighted_values(*items[i - 1], pn_prev)
        pn_prev = pn_cur
    weighted_values(*items[n - 1], pn_prev)


def _attn(q, k2, vt, bias, sink_row):
    bsz, seq, _ = q.shape
    tq = min(Q_TILE, seq)
    nsub = tq // WINDOW
    nblk = seq // WINDOW
    kvw = 2 * KV_WIDTH
    kprev = pl.BlockSpec((None, WINDOW, kvw), lambda b, t: (b, jnp.maximum(t * nsub - 1, 0), 0))
    kcur = pl.BlockSpec((None, tq, kvw), lambda b, t: (b, t, 0))
    knext = pl.BlockSpec((None, WINDOW, kvw), lambda b, t: (b, jnp.minimum((t + 1) * nsub, nblk - 1), 0))
    vprev = pl.BlockSpec((None, KV_WIDTH, WINDOW), lambda b, t: (b, 0, jnp.maximum(t * nsub - 1, 0)))
    vcur = pl.BlockSpec((None, KV_WIDTH, tq), lambda b, t: (b, 0, t))
    vnext = pl.BlockSpec((None, KV_WIDTH, WINDOW), lambda b, t: (b, 0, jnp.minimum((t + 1) * nsub, nblk - 1)))
    return pl.pallas_call(
        functools.partial(_attn_body, nsub),
        grid=(bsz, seq // tq),
        in_specs=[pl.BlockSpec((None, tq, ATTN_WIDTH), lambda b, t: (b, t, 0)),
                  kprev, kcur, knext, vprev, vcur, vnext,
                  _const_spec(bias.shape), _const_spec(sink_row.shape)],
        out_specs=pl.BlockSpec((None, tq, ATTN_WIDTH), lambda b, t: (b, t, 0)),
        out_shape=jax.ShapeDtypeStruct((bsz, seq, ATTN_WIDTH), F32),
        compiler_params=pltpu.CompilerParams(vmem_limit_bytes=VMEM_LIMIT),
        name="window_attention",
    )(q, k2, k2, k2, vt, vt, vt, bias, sink_row)


def _post_body(first, alpha, x_ref, fo_ref, ao_ref, eg_ref, eb_ref, wo_ref, g1_ref, b1_ref,
               w1_ref, w2_ref, g2_ref, b2_ref, o_ref):
    dff = w1_ref.shape[1]
    x = x_ref[...]
    if first:
        x = _layer_norm(x, eg_ref[...], eb_ref[...])
    heads = jnp.concatenate([fo_ref[...], ao_ref[...]], axis=1).astype(BF16)
    mix = jnp.dot(heads, wo_ref[...], preferred_element_type=F32)
    x1 = _layer_norm(alpha * x + mix, g1_ref[...], b1_ref[...])
    x1b = x1.astype(BF16)
    z = None
    for c in range(dff // FF_CHUNK):
        h = jnp.dot(x1b, w1_ref[:, c * FF_CHUNK:(c + 1) * FF_CHUNK], preferred_element_type=F32)
        h = jnp.square(jnp.maximum(h, 0.0)).astype(BF16)
        zc = jnp.dot(h, w2_ref[c * FF_CHUNK:(c + 1) * FF_CHUNK, :], preferred_element_type=F32)
        z = zc if z is None else z + zc
    o_ref[...] = _layer_norm(alpha * x1 + z, g2_ref[...], b2_ref[...])


def _post(x, fo, ao, eg, eb, wo, g1, b1, w1, w2, g2, b2, first, alpha):
    bsz, seq, d = x.shape
    tm = min(ROW_TILE, seq)
    dff = w1.shape[1]
    row = lambda width: pl.BlockSpec((None, tm, width), lambda b, t: (b, t, 0))
    vec = _const_spec((1, d))
    return pl.pallas_call(
        functools.partial(_post_body, first, alpha),
        grid=(bsz, seq // tm),
        in_specs=[row(d), row(FOURIER_WIDTH), row(ATTN_WIDTH), vec, vec,
                  _const_spec((d, d)), vec, vec, _const_spec((d, dff)), _const_spec((dff, d)), vec, vec],
        out_specs=row(d),
        out_shape=jax.ShapeDtypeStruct((bsz, seq, d), F32),
        compiler_params=pltpu.CompilerParams(vmem_limit_bytes=VMEM_LIMIT),
        name="mix_mlp",
    )(x, fo, ao, eg, eb, wo, g1, b1, w1, w2, g2, b2)


def _trunk(x, consts, params):
    cm, a1, a2, bias = consts
    depth = params['w_in'].shape[0]
    alpha = (2.0 * depth) ** 0.25
    eg = params['ln_emb_g'].reshape(1, -1)
    eb = params['ln_emb_b'].reshape(1, -1)
    for l in range(depth):
        first = l == 0
        w_in = params['w_in'][l]
        c2 = FOURIER_WIDTH + ATTN_WIDTH
        kcols = [w_in[:, c2 + h * HEAD_DIM:c2 + (h + 1) * HEAD_DIM] for h in range(N_KV)]
        w_ext = jnp.concatenate([w_in[:, :c2]] + [kcols[0]] * 2 + [kcols[1]] * 2
                                + [w_in[:, c2 + KV_WIDTH:]], axis=1).astype(BF16)
        sink_row = jnp.repeat(params['sink_logits'][l].astype(F32) * LOG2E, WINDOW)
        sink_row = sink_row.reshape(N_KV, 1, Q_PER_KV * WINDOW)
        u, q, k2, vt = _proj(x, eg, eb, w_ext, first)
        y = _f1(u, a1)
        fo = _f2(y, a2, cm, params['w_fourier'][l].astype(BF16))
        ao = _attn(q, k2, vt, bias, sink_row)
        x = _post(x, fo, ao, eg, eb,
                  params['w_out'][l].astype(BF16),
                  params['ln_mix_g'][l].reshape(1, -1), params['ln_mix_b'][l].reshape(1, -1),
                  params['w_ff1'][l].astype(BF16), params['w_ff2'][l].astype(BF16),
                  params['ln_ffn_g'][l].reshape(1, -1), params['ln_ffn_b'][l].reshape(1, -1),
                  first, alpha)
    return x


def kernel(x_prompt, x_sample, ln_emb_g, ln_emb_b, w_in, w_fourier, sink_logits, w_out,
           ln_mix_g, ln_mix_b, w_ff1, w_ff2, ln_ffn_g, ln_ffn_b):
    params = dict(ln_emb_g=ln_emb_g, ln_emb_b=ln_emb_b, w_in=w_in, w_fourier=w_fourier,
                  sink_logits=sink_logits, w_out=w_out, ln_mix_g=ln_mix_g, ln_mix_b=ln_mix_b,
                  w_ff1=w_ff1, w_ff2=w_ff2, ln_ffn_g=ln_ffn_g, ln_ffn_b=ln_ffn_b)
    cm = _channel_dft()
    a1 = _stage1_matrix()
    bias = _attn_bias()
    outs = []
    for x in (x_prompt, x_sample):
        a2 = _stage2_matrices(x.shape[1])
        outs.append(_trunk(x, (cm, a1, a2, bias), params))
    return tuple(outs)
```

```python
import functools
import math

import numpy as np
import jax
import jax.numpy as jnp
from jax import lax
from jax.experimental import pallas as pl
from jax.experimental.pallas import tpu as pltpu

F32 = jnp.float32
BF16 = jnp.bfloat16

FOURIER_WIDTH = 512
N_GROUPS = 4
GROUP = 128
HEAD_DIM = 64
N_HEADS = 8
N_KV = 2
Q_PER_KV = 4
ATTN_WIDTH = 512
KV_WIDTH = 128
WINDOW = 128
LN_EPS = 1e-5
NEG_BIG = -1e30
LOG2E = math.log2(math.e)
SUBLANES = 8
VMEM_LIMIT = 56 * 1024 * 1024

ROW_TILE = 512
Q_TILE = 512
FF_CHUNK = 1024
ATTN_ITEM_HEADS = 2
ATTN_LOOKAHEAD = 3


DFT1 = 256
F1_LANES = 2048
F2_ROWS = 512


def _split(seq):
    s2 = seq // DFT1
    assert s2 * DFT1 == seq and s2 % SUBLANES == 0
    return s2


def _const_spec(shape):
    n = len(shape)
    return pl.BlockSpec(shape, lambda *_: (0,) * n, pipeline_mode=pl.Buffered(1))


def _layer_norm(x, g, b):
    mu = jnp.mean(x, axis=-1, keepdims=True)
    xc = x - mu
    var = jnp.mean(xc * xc, axis=-1, keepdims=True)
    return xc * lax.rsqrt(var + LN_EPS) * g + b


def _fold_rows(x, op):
    r = x.shape[0]
    while r % (2 * SUBLANES) == 0:
        x = op(x[:r // 2], x[r // 2:])
        r //= 2
    return x


def _channel_dft():
    k = np.arange(GROUP)
    ang = 2.0 * np.pi * np.outer(k, k) / GROUP
    cs = np.concatenate([np.cos(ang), np.sin(ang)], axis=0) / np.sqrt(GROUP)
    return jnp.asarray(cs, F32).astype(BF16)


def _stage1_matrix():
    k = np.arange(DFT1)
    ang = 2.0 * np.pi * np.outer(k, k) / DFT1
    a = np.concatenate([np.cos(ang), -np.sin(ang)], axis=0) / np.sqrt(DFT1)
    return jnp.asarray(a, F32).astype(BF16)


def _stage2_matrices(seq):
    s2 = _split(seq)
    r = SUBLANES * s2
    i32 = jnp.int32
    col = jnp.arange(2 * r, dtype=i32)
    part = col // r
    n2c = (col % r) // SUBLANES
    cb = col % SUBLANES
    k2 = jnp.arange(s2, dtype=i32)
    anga = ((k2[:, None] * n2c[None, :]) % s2).astype(F32) * (2.0 * math.pi / s2)
    ca = jnp.cos(anga) * (1.0 / math.sqrt(s2))
    sa = jnp.sin(anga) * (1.0 / math.sqrt(s2))
    pm = jnp.stack([jnp.where(part == 0, ca, sa), jnp.where(part == 0, -sa, ca)])
    qm = jnp.stack([jnp.where(part == 0, -sa, ca), jnp.where(part == 0, -ca, -sa)])
    k1 = jnp.arange(DFT1, dtype=i32)
    angb = ((k1[:, None] * n2c[None, :]) % seq).astype(F32) * (2.0 * math.pi / seq)
    keep = cb[None, :] == (k1 % SUBLANES)[:, None]
    ni = DFT1 // SUBLANES
    cbm = jnp.where(keep, jnp.cos(angb), 0.0).reshape(ni, SUBLANES, 2 * r)
    sbm = jnp.where(keep, jnp.sin(angb), 0.0).reshape(ni, SUBLANES, 2 * r)
    pmx = jnp.broadcast_to(pm[:, :, None, :], (2, s2, SUBLANES, 2 * r)).reshape(2 * r, 2 * r)
    qmx = jnp.broadcast_to(qm[:, :, None, :], (2, s2, SUBLANES, 2 * r)).reshape(2 * r, 2 * r)

    def body(pm_ref, qm_ref, cb_ref, sb_ref, o_ref):
        tiles = (2 * r) // SUBLANES
        pm3 = pm_ref[...].reshape(tiles, SUBLANES, 2 * r)
        qm3 = qm_ref[...].reshape(tiles, SUBLANES, 2 * r)
        val = pm3 * cb_ref[...][None] + qm3 * sb_ref[...][None]
        o_ref[...] = val.reshape(2 * r, 2 * r).astype(BF16)

    item = pl.BlockSpec((None, SUBLANES, 2 * r), lambda i: (i, 0, 0))
    return pl.pallas_call(
        body,
        grid=(ni,),
        in_specs=[_const_spec((2 * r, 2 * r)), _const_spec((2 * r, 2 * r)), item, item],
        out_specs=pl.BlockSpec((None, 2 * r, 2 * r), lambda i: (i, 0, 0)),
        out_shape=jax.ShapeDtypeStruct((ni, 2 * r, 2 * r), BF16),
        compiler_params=pltpu.CompilerParams(vmem_limit_bytes=VMEM_LIMIT),
        name="stage2_matrices",
    )(pmx, qmx, cbm, sbm)


def _attn_bias():
    q = np.arange(WINDOW)[None, :]
    s = np.arange(3 * WINDOW)[:, None]
    rel = s - WINDOW - q
    slopes = 2.0 ** (-8.0 * (np.arange(N_HEADS) + 1.0) / N_HEADS)
    out = np.empty((N_KV, 3 * WINDOW, Q_PER_KV * WINDOW), np.float32)
    for h in range(N_HEADS):
        b = np.where(np.abs(rel) <= WINDOW, -slopes[h] * np.abs(rel) * LOG2E, NEG_BIG)
        kh, g = divmod(h, Q_PER_KV)
        out[kh, :, g * WINDOW:(g + 1) * WINDOW] = b
    return jnp.asarray(out)


def _proj_body(first, s2, x_ref, eg_ref, eb_ref, w_ref, u_ref, q_ref, k_ref, vt_ref, us_scr):
    x = x_ref[...]
    if first:
        x = _layer_norm(x, eg_ref[...], eb_ref[...])
    p = jnp.dot(x.astype(BF16), w_ref[...], preferred_element_type=F32)
    c1 = FOURIER_WIDTH
    c2 = c1 + ATTN_WIDTH
    c3 = c2 + 2 * KV_WIDTH
    q_ref[...] = (p[:, c1:c2] * (LOG2E / math.sqrt(HEAD_DIM))).astype(BF16)
    k_ref[...] = p[:, c2:c3].astype(BF16)
    vt_ref[...] = p[:, c3:].T.astype(BF16)
    nrow = x_ref.shape[0] // s2
    pitch = s2 + SUBLANES
    for g in range(N_GROUPS):
        for n1 in range(nrow):
            us_scr[g, n1 * pitch:n1 * pitch + s2, :] = p[n1 * s2:(n1 + 1) * s2, g * GROUP:(g + 1) * GROUP]
    for n2 in range(s2):
        for g in range(N_GROUPS):
            piece = us_scr[g, pl.ds(n2, nrow, stride=pitch), :]
            lo = n2 * FOURIER_WIDTH + g * GROUP
            u_ref[:, lo:lo + GROUP] = piece.astype(BF16)


def _proj(x, eg, eb, w_ext, first):
    bsz, seq, d = x.shape
    s2 = _split(seq)
    tm = max(ROW_TILE, 16 * s2)
    wn = w_ext.shape[1]
    row = lambda width: pl.BlockSpec((None, tm, width), lambda b, t: (b, t, 0))
    return pl.pallas_call(
        functools.partial(_proj_body, first, s2),
        grid=(bsz, seq // tm),
        in_specs=[row(d), _const_spec((1, d)), _const_spec((1, d)), _const_spec((d, wn))],
        out_specs=[pl.BlockSpec((None, tm // s2, s2 * FOURIER_WIDTH), lambda b, t: (b, t, 0)),
                   row(ATTN_WIDTH), row(2 * KV_WIDTH),
                   pl.BlockSpec((None, KV_WIDTH, tm), lambda b, t: (b, 0, t))],
        out_shape=[jax.ShapeDtypeStruct((bsz, DFT1, s2 * FOURIER_WIDTH), BF16),
                   jax.ShapeDtypeStruct((bsz, seq, ATTN_WIDTH), BF16),
                   jax.ShapeDtypeStruct((bsz, seq, 2 * KV_WIDTH), BF16),
                   jax.ShapeDtypeStruct((bsz, KV_WIDTH, seq), BF16)],
        scratch_shapes=[pltpu.VMEM((N_GROUPS, (tm // s2) * (s2 + SUBLANES), GROUP), F32)],
        compiler_params=pltpu.CompilerParams(vmem_limit_bytes=VMEM_LIMIT),
        name="proj",
    )(x, eg, eb, w_ext)


def _f1_body(u_ref, a1_ref, y_ref):
    y = jnp.dot(a1_ref[...], u_ref[...], preferred_element_type=F32).astype(BF16)
    y_ref[0] = y[:DFT1]
    y_ref[1] = y[DFT1:]


def _f1(u2, a1):
    bsz, _, lanes = u2.shape
    lb = min(F1_LANES, lanes)
    return pl.pallas_call(
        _f1_body,
        grid=(bsz, lanes // lb),
        in_specs=[pl.BlockSpec((None, DFT1, lb), lambda b, j: (b, 0, j)), _const_spec(a1.shape)],
        out_specs=pl.BlockSpec((None, 2, DFT1, lb), lambda b, j: (b, 0, 0, j)),
        out_shape=jax.ShapeDtypeStruct((bsz, 2, DFT1, lanes), BF16),
        compiler_params=pltpu.CompilerParams(vmem_limit_bytes=VMEM_LIMIT),
        name="fourier_stage1",
    )(u2, a1)


def _f2_body(nitem, y_ref, a2_ref, cm_ref, wf_ref, o_ref):
    width = FOURIER_WIDTH
    s2 = y_ref.shape[2] // width
    r = SUBLANES * s2
    pr, pi = [], []
    yf = y_ref[...].astype(F32)
    for c in range(nitem):
        rs = slice(c * SUBLANES, (c + 1) * SUBLANES)
        ys = jnp.concatenate([yf[part, rs, n2 * width:(n2 + 1) * width]
                              for part in range(2) for n2 in range(s2)], axis=0).astype(BF16)
        p = jnp.dot(a2_ref[c], ys, preferred_element_type=F32)
        pr.append(p[:r])
        pi.append(p[r:])
    pr = jnp.concatenate(pr, axis=0).astype(BF16)
    pi = jnp.concatenate(pi, axis=0).astype(BF16)
    rows = nitem * r
    both = jnp.concatenate(
        [jnp.concatenate([pr[:, g * GROUP:(g + 1) * GROUP], pi[:, g * GROUP:(g + 1) * GROUP]], axis=1)
         for g in range(N_GROUPS)], axis=0)
    f = jnp.dot(both, cm_ref[...], preferred_element_type=F32).astype(BF16)
    outs = [jnp.dot(f[g * rows:(g + 1) * rows], wf_ref[g], preferred_element_type=F32)
            for g in range(N_GROUPS)]
    out = jnp.concatenate(outs, axis=1)
    for c in range(nitem):
        o_ref[:, c * SUBLANES:(c + 1) * SUBLANES, :] = out[c * r:(c + 1) * r].reshape(s2, SUBLANES, FOURIER_WIDTH)


def _f2(y, a2, cm, wf):
    bsz, _, _, lanes = y.shape
    width = FOURIER_WIDTH
    s2 = lanes // width
    seq = DFT1 * s2
    r = SUBLANES * s2
    nitem = max(2, F2_ROWS // r)
    nm = DFT1 // (SUBLANES * nitem)
    out = pl.pallas_call(
        functools.partial(_f2_body, nitem),
        grid=(nm, bsz),
        in_specs=[pl.BlockSpec((None, 2, nitem * SUBLANES, lanes), lambda m, b: (b, 0, m, 0)),
                  pl.BlockSpec((nitem, 2 * r, 2 * r), lambda m, b: (m, 0, 0)),
                  _const_spec(cm.shape), _const_spec(wf.shape)],
        out_specs=pl.BlockSpec((None, s2, None, nitem * SUBLANES, width), lambda m, b: (b, 0, m, 0, 0)),
        out_shape=jax.ShapeDtypeStruct((bsz, s2, nm, nitem * SUBLANES, width), F32),
        compiler_params=pltpu.CompilerParams(vmem_limit_bytes=VMEM_LIMIT),
        name="fourier_stage2",
    )(y, a2, cm, wf)
    return out.reshape(bsz, seq, width)


def _attn_body(nsub, q_ref, kp_ref, kc_ref, kn_ref, vp_ref, vc_ref, vn_ref, bias_ref, sink_ref, o_ref):
    t = pl.program_id(1)
    last = pl.num_programs(1) - 1
    kcat = jnp.concatenate([kp_ref[...], kc_ref[...], kn_ref[...]], axis=0)
    vtcat = jnp.concatenate([vp_ref[...], vc_ref[...], vn_ref[...]], axis=1)
    neg_first = jnp.where(t == 0, NEG_BIG, 0.0).astype(F32)
    neg_last = jnp.where(t == last, NEG_BIG, 0.0).astype(F32)
    lane = lax.broadcasted_iota(jnp.int32, (WINDOW, 2 * HEAD_DIM), 1)
    m_left = (lane < HEAD_DIM).astype(BF16)
    m_right = 1 - m_left
    pairs_per_item = ATTN_ITEM_HEADS // 2
    items = [(s, kh, p0) for s in range(nsub) for kh in range(N_KV)
             for p0 in range(0, Q_PER_KV // 2, pairs_per_item)]
    width = ATTN_ITEM_HEADS * WINDOW

    def scores(s, kh, p0):
        r0 = s * WINDOW
        parts = []
        for pair in range(p0, p0 + pairs_per_item):
            c0 = (kh * 2 + pair) * GROUP
            qa = q_ref[r0:r0 + WINDOW, c0:c0 + GROUP]
            parts += [qa * m_left, qa * m_right]
        st = jnp.concatenate(parts, axis=0)
        ks = kcat[r0:r0 + 3 * WINDOW, kh * GROUP:(kh + 1) * GROUP]
        sc = lax.dot_general(ks, st, (((1,), (1,)), ((), ())), preferred_element_type=F32)
        sc = sc + bias_ref[kh, :, 2 * p0 * WINDOW:2 * p0 * WINDOW + width]
        if s == 0:
            sc = jnp.concatenate([sc[:WINDOW] + neg_first, sc[WINDOW:]], axis=0)
        if s == nsub - 1:
            sc = jnp.concatenate([sc[:2 * WINDOW], sc[2 * WINDOW:] + neg_last], axis=0)
        return sc

    def softmax(s, kh, p0, sc):
        sink = sink_ref[kh, :, 2 * p0 * WINDOW:2 * p0 * WINDOW + width]
        m = jnp.maximum(jnp.max(_fold_rows(sc, jnp.maximum), axis=0, keepdims=True), sink)
        p = jnp.exp2(sc - m)
        denom = jnp.sum(_fold_rows(p, jnp.add), axis=0, keepdims=True) + jnp.exp2(sink - m)
        return p.astype(BF16), 1.0 / denom

    def weighted_values(s, kh, p0, pn_inv):
        pn, inv = pn_inv
        r0 = s * WINDOW
        vts = vtcat[kh * HEAD_DIM:(kh + 1) * HEAD_DIM, r0:r0 + 3 * WINDOW]
        ot = jnp.dot(vts, pn, preferred_element_type=F32) * inv
        for j in range(pairs_per_item):
            c0 = (kh * 2 + p0 + j) * GROUP
            two = jnp.concatenate([ot[:, 2 * j * WINDOW:(2 * j + 1) * WINDOW],
                                   ot[:, (2 * j + 1) * WINDOW:(2 * j + 2) * WINDOW]], axis=0)
            o_ref[r0:r0 + WINDOW, c0:c0 + GROUP] = two.T

    n = len(items)
    sc = {i: scores(*items[i]) for i in range(min(ATTN_LOOKAHEAD, n))}
    pn_prev = None
    for i in range(n):
        if i + ATTN_LOOKAHEAD < n:
            sc[i + ATTN_LOOKAHEAD] = scores(*items[i + ATTN_LOOKAHEAD])
        pn_cur = softmax(*items[i], sc.pop(i))
        if i >= 1:
            weighted_values(*items[i - 1], pn_prev)
        pn_prev = pn_cur
    weighted_values(*items[n - 1], pn_prev)


def _attn(q, k2, vt, bias, sink_row):
    bsz, seq, _ = q.shape
    tq = min(Q_TILE, seq)
    nsub = tq // WINDOW
    nblk = seq // WINDOW
    kvw = 2 * KV_WIDTH
    kprev = pl.BlockSpec((None, WINDOW, kvw), lambda b, t: (b, jnp.maximum(t * nsub - 1, 0), 0))
    kcur = pl.BlockSpec((None, tq, kvw), lambda b, t: (b, t, 0))
    knext = pl.BlockSpec((None, WINDOW, kvw), lambda b, t: (b, jnp.minimum((t + 1) * nsub, nblk - 1), 0))
    vprev = pl.BlockSpec((None, KV_WIDTH, WINDOW), lambda b, t: (b, 0, jnp.maximum(t * nsub - 1, 0)))
    vcur = pl.BlockSpec((None, KV_WIDTH, tq), lambda b, t: (b, 0, t))
    vnext = pl.BlockSpec((None, KV_WIDTH, WINDOW), lambda b, t: (b, 0, jnp.minimum((t + 1) * nsub, nblk - 1)))
    return pl.pallas_call(
        functools.partial(_attn_body, nsub),
        grid=(bsz, seq // tq),
        in_specs=[pl.BlockSpec((None, tq, ATTN_WIDTH), lambda b, t: (b, t, 0)),
                  kprev, kcur, knext, vprev, vcur, vnext,
                  _const_spec(bias.shape), _const_spec(sink_row.shape)],
        out_specs=pl.BlockSpec((None, tq, ATTN_WIDTH), lambda b, t: (b, t, 0)),
        out_shape=jax.ShapeDtypeStruct((bsz, seq, ATTN_WIDTH), F32),
        compiler_params=pltpu.CompilerParams(vmem_limit_bytes=VMEM_LIMIT),
        name="window_attention",
    )(q, k2, k2, k2, vt, vt, vt, bias, sink_row)


def _post_body(first, alpha, x_ref, fo_ref, ao_ref, eg_ref, eb_ref, wo_ref, g1_ref, b1_ref,
               w1_ref, w2_ref, g2_ref, b2_ref, o_ref):
    dff = w1_ref.shape[1]
    x = x_ref[...]
    if first:
        x = _layer_norm(x, eg_ref[...], eb_ref[...])
    heads = jnp.concatenate([fo_ref[...], ao_ref[...]], axis=1).astype(BF16)
    mix = jnp.dot(heads, wo_ref[...], preferred_element_type=F32)
    x1 = _layer_norm(alpha * x + mix, g1_ref[...], b1_ref[...])
    x1b = x1.astype(BF16)
    z = None
    for c in range(dff // FF_CHUNK):
        h = jnp.dot(x1b, w1_ref[:, c * FF_CHUNK:(c + 1) * FF_CHUNK], preferred_element_type=F32)
        h = jnp.square(jnp.maximum(h, 0.0)).astype(BF16)
        zc = jnp.dot(h, w2_ref[c * FF_CHUNK:(c + 1) * FF_CHUNK, :], preferred_element_type=F32)
        z = zc if z is None else z + zc
    o_ref[...] = _layer_norm(alpha * x1 + z, g2_ref[...], b2_ref[...])


def _post(x, fo, ao, eg, eb, wo, g1, b1, w1, w2, g2, b2, first, alpha):
    bsz, seq, d = x.shape
    tm = min(ROW_TILE, seq)
    dff = w1.shape[1]
    row = lambda width: pl.BlockSpec((None, tm, width), lambda b, t: (b, t, 0))
    vec = _const_spec((1, d))
    return pl.pallas_call(
        functools.partial(_post_body, first, alpha),
        grid=(bsz, seq // tm),
        in_specs=[row(d), row(FOURIER_WIDTH), row(ATTN_WIDTH), vec, vec,
                  _const_spec((d, d)), vec, vec, _const_spec((d, dff)), _const_spec((dff, d)), vec, vec],
        out_specs=row(d),
        out_shape=jax.ShapeDtypeStruct((bsz, seq, d), F32),
        compiler_params=pltpu.CompilerParams(vmem_limit_bytes=VMEM_LIMIT),
        name="mix_mlp",
    )(x, fo, ao, eg, eb, wo, g1, b1, w1, w2, g2, b2)


def _trunk(x, consts, params):
    cm, a1, a2, bias = consts
    depth = params['w_in'].shape[0]
    alpha = (2.0 * depth) ** 0.25
    eg = params['ln_emb_g'].reshape(1, -1)
    eb = params['ln_emb_b'].reshape(1, -1)
    for l in range(depth):
        first = l == 0
        w_in = params['w_in'][l]
        c2 = FOURIER_WIDTH + ATTN_WIDTH
        kcols = [w_in[:, c2 + h * HEAD_DIM:c2 + (h + 1) * HEAD_DIM] for h in range(N_KV)]
        w_ext = jnp.concatenate([w_in[:, :c2]] + [kcols[0]] * 2 + [kcols[1]] * 2
                                + [w_in[:, c2 + KV_WIDTH:]], axis=1).astype(BF16)
        sink_row = jnp.repeat(params['sink_logits'][l].astype(F32) * LOG2E, WINDOW)
        sink_row = sink_row.reshape(N_KV, 1, Q_PER_KV * WINDOW)
        u, q, k2, vt = _proj(x, eg, eb, w_ext, first)
        y = _f1(u, a1)
        fo = _f2(y, a2, cm, params['w_fourier'][l].astype(BF16))
        ao = _attn(q, k2, vt, bias, sink_row)
        x = _post(x, fo, ao, eg, eb,
                  params['w_out'][l].astype(BF16),
                  params['ln_mix_g'][l].reshape(1, -1), params['ln_mix_b'][l].reshape(1, -1),
                  params['w_ff1'][l].astype(BF16), params['w_ff2'][l].astype(BF16),
                  params['ln_ffn_g'][l].reshape(1, -1), params['ln_ffn_b'][l].reshape(1, -1),
                  first, alpha)
    return x


def kernel(x_prompt, x_sample, ln_emb_g, ln_emb_b, w_in, w_fourier, sink_logits, w_out,
           ln_mix_g, ln_mix_b, w_ff1, w_ff2, ln_ffn_g, ln_ffn_b):
    params = dict(ln_emb_g=ln_emb_g, ln_emb_b=ln_emb_b, w_in=w_in, w_fourier=w_fourier,
                  sink_logits=sink_logits, w_out=w_out, ln_mix_g=ln_mix_g, ln_mix_b=ln_mix_b,
                  w_ff1=w_ff1, w_ff2=w_ff2, ln_ffn_g=ln_ffn_g, ln_ffn_b=ln_ffn_b)
    cm = _channel_dft()
    a1 = _stage1_matrix()
    bias = _attn_bias()
    outs = []
    for x in (x_prompt, x_sample):
        a2 = _stage2_matrices(x.shape[1])
        outs.append(_trunk(x, (cm, a1, a2, bias), params))
    return tuple(outs)
```

```python
import functools
import math

import numpy as np
import jax
import jax.numpy as jnp
from jax import lax
from jax.experimental import pallas as pl
from jax.experimental.pallas import tpu as pltpu

F32 = jnp.float32
BF16 = jnp.bfloat16

FOURIER_WIDTH = 512
N_GROUPS = 4
GROUP = 128
HEAD_DIM = 64
N_HEADS = 8
N_KV = 2
Q_PER_KV = 4
ATTN_WIDTH = 512
KV_WIDTH = 128
WINDOW = 128
LN_EPS = 1e-5
NEG_BIG = -1e30
LOG2E = math.log2(math.e)
SUBLANES = 8
VMEM_LIMIT = 56 * 1024 * 1024

ROW_TILE = 512
Q_TILE = 1024
FF_CHUNK = 1024
ATTN_ITEM_HEADS = 2
ATTN_LOOKAHEAD = 3


DFT1 = 256
F1_LANES = 4096
F2_ROWS = 1024


def _split(seq):
    s2 = seq // DFT1
    assert s2 * DFT1 == seq and s2 % SUBLANES == 0
    return s2


def _const_spec(shape):
    n = len(shape)
    return pl.BlockSpec(shape, lambda *_: (0,) * n, pipeline_mode=pl.Buffered(1))


def _layer_norm(x, g, b):
    mu = jnp.mean(x, axis=-1, keepdims=True)
    xc = x - mu
    var = jnp.mean(xc * xc, axis=-1, keepdims=True)
    return xc * lax.rsqrt(var + LN_EPS) * g + b


def _fold_rows(x, op):
    r = x.shape[0]
    while r % (2 * SUBLANES) == 0:
        x = op(x[:r // 2], x[r // 2:])
        r //= 2
    return x


def _channel_dft():
    k = np.arange(GROUP)
    ang = 2.0 * np.pi * np.outer(k, k) / GROUP
    cs = np.concatenate([np.cos(ang), np.sin(ang)], axis=0) / np.sqrt(GROUP)
    return jnp.asarray(cs, F32).astype(BF16)


def _stage1_matrix():
    k = np.arange(DFT1)
    ang = 2.0 * np.pi * np.outer(k, k) / DFT1
    a = np.concatenate([np.cos(ang), -np.sin(ang)], axis=0) / np.sqrt(DFT1)
    return jnp.asarray(a, F32).astype(BF16)


def _stage2_matrices(seq):
    s2 = _split(seq)
    r = SUBLANES * s2
    i32 = jnp.int32
    col = jnp.arange(2 * r, dtype=i32)
    part = col // r
    n2c = (col % r) // SUBLANES
    cb = col % SUBLANES
    k2 = jnp.arange(s2, dtype=i32)
    anga = ((k2[:, None] * n2c[None, :]) % s2).astype(F32) * (2.0 * math.pi / s2)
    ca = jnp.cos(anga) * (1.0 / math.sqrt(s2))
    sa = jnp.sin(anga) * (1.0 / math.sqrt(s2))
    pm = jnp.stack([jnp.where(part == 0, ca, sa), jnp.where(part == 0, -sa, ca)])
    qm = jnp.stack([jnp.where(part == 0, -sa, ca), jnp.where(part == 0, -ca, -sa)])
    k1 = jnp.arange(DFT1, dtype=i32)
    angb = ((k1[:, None] * n2c[None, :]) % seq).astype(F32) * (2.0 * math.pi / seq)
    keep = cb[None, :] == (k1 % SUBLANES)[:, None]
    ni = DFT1 // SUBLANES
    cbm = jnp.where(keep, jnp.cos(angb), 0.0).reshape(ni, SUBLANES, 2 * r)
    sbm = jnp.where(keep, jnp.sin(angb), 0.0).reshape(ni, SUBLANES, 2 * r)
    pmx = jnp.broadcast_to(pm[:, :, None, :], (2, s2, SUBLANES, 2 * r)).reshape(2 * r, 2 * r)
    qmx = jnp.broadcast_to(qm[:, :, None, :], (2, s2, SUBLANES, 2 * r)).reshape(2 * r, 2 * r)

    def body(pm_ref, qm_ref, cb_ref, sb_ref, o_ref):
        tiles = (2 * r) // SUBLANES
        pm3 = pm_ref[...].reshape(tiles, SUBLANES, 2 * r)
        qm3 = qm_ref[...].reshape(tiles, SUBLANES, 2 * r)
        val = pm3 * cb_ref[...][None] + qm3 * sb_ref[...][None]
        o_ref[...] = val.reshape(2 * r, 2 * r).astype(BF16)

    item = pl.BlockSpec((None, SUBLANES, 2 * r), lambda i: (i, 0, 0))
    return pl.pallas_call(
        body,
        grid=(ni,),
        in_specs=[_const_spec((2 * r, 2 * r)), _const_spec((2 * r, 2 * r)), item, item],
        out_specs=pl.BlockSpec((None, 2 * r, 2 * r), lambda i: (i, 0, 0)),
        out_shape=jax.ShapeDtypeStruct((ni, 2 * r, 2 * r), BF16),
        compiler_params=pltpu.CompilerParams(vmem_limit_bytes=VMEM_LIMIT),
        name="stage2_matrices",
    )(pmx, qmx, cbm, sbm)


def _attn_bias():
    q = np.arange(WINDOW)[None, :]
    s = np.arange(3 * WINDOW)[:, None]
    rel = s - WINDOW - q
    slopes = 2.0 ** (-8.0 * (np.arange(N_HEADS) + 1.0) / N_HEADS)
    out = np.empty((N_KV, 3 * WINDOW, Q_PER_KV * WINDOW), np.float32)
    for h in range(N_HEADS):
        b = np.where(np.abs(rel) <= WINDOW, -slopes[h] * np.abs(rel) * LOG2E, NEG_BIG)
        kh, g = divmod(h, Q_PER_KV)
        out[kh, :, g * WINDOW:(g + 1) * WINDOW] = b
    return jnp.asarray(out)


def _proj_body(first, s2, x_ref, eg_ref, eb_ref, w_ref, u_ref, q_ref, k_ref, vt_ref, us_scr):
    x = x_ref[...]
    if first:
        x = _layer_norm(x, eg_ref[...], eb_ref[...])
    p = jnp.dot(x.astype(BF16), w_ref[...], preferred_element_type=F32)
    c1 = FOURIER_WIDTH
    c2 = c1 + ATTN_WIDTH
    c3 = c2 + 2 * KV_WIDTH
    q_ref[...] = (p[:, c1:c2] * (LOG2E / math.sqrt(HEAD_DIM))).astype(BF16)
    k_ref[...] = p[:, c2:c3].astype(BF16)
    vt_ref[...] = p[:, c3:].T.astype(BF16)
    nrow = x_ref.shape[0] // s2
    pitch = s2 + SUBLANES
    for g in range(N_GROUPS):
        for n1 in range(nrow):
            us_scr[g, n1 * pitch:n1 * pitch + s2, :] = p[n1 * s2:(n1 + 1) * s2, g * GROUP:(g + 1) * GROUP]
    for n2 in range(s2):
        for g in range(N_GROUPS):
            piece = us_scr[g, pl.ds(n2, nrow, stride=pitch), :]
            lo = n2 * FOURIER_WIDTH + g * GROUP
            u_ref[:, lo:lo + GROUP] = piece.astype(BF16)


def _proj(x, eg, eb, w_ext, first):
    bsz, seq, d = x.shape
    s2 = _split(seq)
    tm = max(ROW_TILE, 16 * s2)
    wn = w_ext.shape[1]
    row = lambda width: pl.BlockSpec((None, tm, width), lambda b, t: (b, t, 0))
    return pl.pallas_call(
        functools.partial(_proj_body, first, s2),
        grid=(bsz, seq // tm),
        in_specs=[row(d), _const_spec((1, d)), _const_spec((1, d)), _const_spec((d, wn))],
        out_specs=[pl.BlockSpec((None, tm // s2, s2 * FOURIER_WIDTH), lambda b, t: (b, t, 0)),
                   row(ATTN_WIDTH), row(2 * KV_WIDTH),
                   pl.BlockSpec((None, KV_WIDTH, tm), lambda b, t: (b, 0, t))],
        out_shape=[jax.ShapeDtypeStruct((bsz, DFT1, s2 * FOURIER_WIDTH), BF16),
                   jax.ShapeDtypeStruct((bsz, seq, ATTN_WIDTH), BF16),
                   jax.ShapeDtypeStruct((bsz, seq, 2 * KV_WIDTH), BF16),
                   jax.ShapeDtypeStruct((bsz, KV_WIDTH, seq), BF16)],
        scratch_shapes=[pltpu.VMEM((N_GROUPS, (tm // s2) * (s2 + SUBLANES), GROUP), F32)],
        compiler_params=pltpu.CompilerParams(vmem_limit_bytes=VMEM_LIMIT),
        name="proj",
    )(x, eg, eb, w_ext)


def _f1_body(u_ref, a1_ref, y_ref):
    y = jnp.dot(a1_ref[...], u_ref[...], preferred_element_type=F32).astype(BF16)
    y_ref[0] = y[:DFT1]
    y_ref[1] = y[DFT1:]


def _f1(u2, a1):
    bsz, _, lanes = u2.shape
    lb = min(F1_LANES, lanes)
    return pl.pallas_call(
        _f1_body,
        grid=(bsz, lanes // lb),
        in_specs=[pl.BlockSpec((None, DFT1, lb), lambda b, j: (b, 0, j)), _const_spec(a1.shape)],
        out_specs=pl.BlockSpec((None, 2, DFT1, lb), lambda b, j: (b, 0, 0, j)),
        out_shape=jax.ShapeDtypeStruct((bsz, 2, DFT1, lanes), BF16),
        compiler_params=pltpu.CompilerParams(vmem_limit_bytes=VMEM_LIMIT),
        name="fourier_stage1",
    )(u2, a1)


def _f2_body(nitem, y_ref, a2_ref, cm_ref, wf_ref, o_ref):
    width = FOURIER_WIDTH
    s2 = y_ref.shape[2] // width
    r = SUBLANES * s2
    pr, pi = [], []
    yf = y_ref[...].astype(F32)
    for c in range(nitem):
        rs = slice(c * SUBLANES, (c + 1) * SUBLANES)
        ys = jnp.concatenate([yf[part, rs, n2 * width:(n2 + 1) * width]
                              for part in range(2) for n2 in range(s2)], axis=0).astype(BF16)
        p = jnp.dot(a2_ref[c], ys, preferred_element_type=F32)
        pr.append(p[:r])
        pi.append(p[r:])
    pr = jnp.concatenate(pr, axis=0).astype(BF16)
    pi = jnp.concatenate(pi, axis=0).astype(BF16)
    rows = nitem * r
    both = jnp.concatenate(
        [jnp.concatenate([pr[:, g * GROUP:(g + 1) * GROUP], pi[:, g * GROUP:(g + 1) * GROUP]], axis=1)
         for g in range(N_GROUPS)], axis=0)
    f = jnp.dot(both, cm_ref[...], preferred_element_type=F32).astype(BF16)
    outs = [jnp.dot(f[g * rows:(g + 1) * rows], wf_ref[g], preferred_element_type=F32)
            for g in range(N_GROUPS)]
    out = jnp.concatenate(outs, axis=1)
    for c in range(nitem):
        o_ref[:, c * SUBLANES:(c + 1) * SUBLANES, :] = out[c * r:(c + 1) * r].reshape(s2, SUBLANES, FOURIER_WIDTH)


def _f2(y, a2, cm, wf):
    bsz, _, _, lanes = y.shape
    width = FOURIER_WIDTH
    s2 = lanes // width
    seq = DFT1 * s2
    r = SUBLANES * s2
    nitem = max(2, F2_ROWS // r)
    nm = DFT1 // (SUBLANES * nitem)
    out = pl.pallas_call(
        functools.partial(_f2_body, nitem),
        grid=(nm, bsz),
        in_specs=[pl.BlockSpec((None, 2, nitem * SUBLANES, lanes), lambda m, b: (b, 0, m, 0)),
                  pl.BlockSpec((nitem, 2 * r, 2 * r), lambda m, b: (m, 0, 0)),
                  _const_spec(cm.shape), _const_spec(wf.shape)],
        out_specs=pl.BlockSpec((None, s2, None, nitem * SUBLANES, width), lambda m, b: (b, 0, m, 0, 0)),
        out_shape=jax.ShapeDtypeStruct((bsz, s2, nm, nitem * SUBLANES, width), F32),
        compiler_params=pltpu.CompilerParams(vmem_limit_bytes=VMEM_LIMIT),
        name="fourier_stage2",
    )(y, a2, cm, wf)
    return out.reshape(bsz, seq, width)


def _attn_body(nsub, q_ref, kp_ref, kc_ref, kn_ref, vp_ref, vc_ref, vn_ref, bias_ref, sink_ref, o_ref):
    t = pl.program_id(1)
    last = pl.num_programs(1) - 1
    kcat = jnp.concatenate([kp_ref[...], kc_ref[...], kn_ref[...]], axis=0)
    vtcat = jnp.concatenate([vp_ref[...], vc_ref[...], vn_ref[...]], axis=1)
    neg_first = jnp.where(t == 0, NEG_BIG, 0.0).astype(F32)
    neg_last = jnp.where(t == last, NEG_BIG, 0.0).astype(F32)
    lane = lax.broadcasted_iota(jnp.int32, (WINDOW, 2 * HEAD_DIM), 1)
    m_left = (lane < HEAD_DIM).astype(BF16)
    m_right = 1 - m_left
    pairs_per_item = ATTN_ITEM_HEADS // 2
    items = [(s, kh, p0) for s in range(nsub) for kh in range(N_KV)
             for p0 in range(0, Q_PER_KV // 2, pairs_per_item)]
    width = ATTN_ITEM_HEADS * WINDOW

    def scores(s, kh, p0):
        r0 = s * WINDOW
        parts = []
        for pair in range(p0, p0 + pairs_per_item):
            c0 = (kh * 2 + pair) * GROUP
            qa = q_ref[r0:r0 + WINDOW, c0:c0 + GROUP]
            parts += [qa * m_left, qa * m_right]
        st = jnp.concatenate(parts, axis=0)
        ks = kcat[r0:r0 + 3 * WINDOW, kh * GROUP:(kh + 1) * GROUP]
        sc = lax.dot_general(ks, st, (((1,), (1,)), ((), ())), preferred_element_type=F32)
        sc = sc + bias_ref[kh, :, 2 * p0 * WINDOW:2 * p0 * WINDOW + width]
        if s == 0:
            sc = jnp.concatenate([sc[:WINDOW] + neg_first, sc[WINDOW:]], axis=0)
        if s == nsub - 1:
            sc = jnp.concatenate([sc[:2 * WINDOW], sc[2 * WINDOW:] + neg_last], axis=0)
        return sc

    def softmax(s, kh, p0, sc):
        sink = sink_ref[kh, :, 2 * p0 * WINDOW:2 * p0 * WINDOW + width]
        m = jnp.maximum(jnp.max(_fold_rows(sc, jnp.maximum), axis=0, keepdims=True), sink)
        p = jnp.exp2(sc - m)
        denom = jnp.sum(_fold_rows(p, jnp.add), axis=0, keepdims=True) + jnp.exp2(sink - m)
        return p.astype(BF16), 1.0 / denom

    def weighted_values(s, kh, p0, pn_inv):
        pn, inv = pn_inv
        r0 = s * WINDOW
        vts = vtcat[kh * HEAD_DIM:(kh + 1) * HEAD_DIM, r0:r0 + 3 * WINDOW]
        ot = jnp.dot(vts, pn, preferred_element_type=F32) * inv
        for j in range(pairs_per_item):
            c0 = (kh * 2 + p0 + j) * GROUP
            two = jnp.concatenate([ot[:, 2 * j * WINDOW:(2 * j + 1) * WINDOW],
                                   ot[:, (2 * j + 1) * WINDOW:(2 * j + 2) * WINDOW]], axis=0)
            o_ref[r0:r0 + WINDOW, c0:c0 + GROUP] = two.T

    n = len(items)
    sc = {i: scores(*items[i]) for i in range(min(ATTN_LOOKAHEAD, n))}
    pn_prev = None
    for i in range(n):
        if i + ATTN_LOOKAHEAD < n:
            sc[i + ATTN_LOOKAHEAD] = scores(*items[i + ATTN_LOOKAHEAD])
        pn_cur = softmax(*items[i], sc.pop(i))
        if i >= 1:
            weighted_values(*items[i - 1], pn_prev)
        pn_prev = pn_cur
    weighted_values(*items[n - 1], pn_prev)


def _attn(q, k2, vt, bias, sink_row):
    bsz, seq, _ = q.shape
    tq = min(Q_TILE, seq)
    nsub = tq // WINDOW
    nblk = seq // WINDOW
    kvw = 2 * KV_WIDTH
    kprev = pl.BlockSpec((None, WINDOW, kvw), lambda b, t: (b, jnp.maximum(t * nsub - 1, 0), 0))
    kcur = pl.BlockSpec((None, tq, kvw), lambda b, t: (b, t, 0))
    knext = pl.BlockSpec((None, WINDOW, kvw), lambda b, t: (b, jnp.minimum((t + 1) * nsub, nblk - 1), 0))
    vprev = pl.BlockSpec((None, KV_WIDTH, WINDOW), lambda b, t: (b, 0, jnp.maximum(t * nsub - 1, 0)))
    vcur = pl.BlockSpec((None, KV_WIDTH, tq), lambda b, t: (b, 0, t))
    vnext = pl.BlockSpec((None, KV_WIDTH, WINDOW), lambda b, t: (b, 0, jnp.minimum((t + 1) * nsub, nblk - 1)))
    return pl.pallas_call(
        functools.partial(_attn_body, nsub),
        grid=(bsz, seq // tq),
        in_specs=[pl.BlockSpec((None, tq, ATTN_WIDTH), lambda b, t: (b, t, 0)),
                  kprev, kcur, knext, vprev, vcur, vnext,
                  _const_spec(bias.shape), _const_spec(sink_row.shape)],
        out_specs=pl.BlockSpec((None, tq, ATTN_WIDTH), lambda b, t: (b, t, 0)),
        out_shape=jax.ShapeDtypeStruct((bsz, seq, ATTN_WIDTH), F32),
        compiler_params=pltpu.CompilerParams(vmem_limit_bytes=VMEM_LIMIT),
        name="window_attention",
    )(q, k2, k2, k2, vt, vt, vt, bias, sink_row)


def _post_body(first, alpha, x_ref, fo_ref, ao_ref, eg_ref, eb_ref, wo_ref, g1_ref, b1_ref,
               w1_ref, w2_ref, g2_ref, b2_ref, o_ref):
    dff = w1_ref.shape[1]
    x = x_ref[...]
    if first:
        x = _layer_norm(x, eg_ref[...], eb_ref[...])
    heads = jnp.concatenate([fo_ref[...], ao_ref[...]], axis=1).astype(BF16)
    mix = jnp.dot(heads, wo_ref[...], preferred_element_type=F32)
    x1 = _layer_norm(alpha * x + mix, g1_ref[...], b1_ref[...])
    x1b = x1.astype(BF16)
    z = None
    for c in range(dff // FF_CHUNK):
        h = jnp.dot(x1b, w1_ref[:, c * FF_CHUNK:(c + 1) * FF_CHUNK], preferred_element_type=F32)
        h = jnp.square(jnp.maximum(h, 0.0)).astype(BF16)
        zc = jnp.dot(h, w2_ref[c * FF_CHUNK:(c + 1) * FF_CHUNK, :], preferred_element_type=F32)
        z = zc if z is None else z + zc
    o_ref[...] = _layer_norm(alpha * x1 + z, g2_ref[...], b2_ref[...])


def _post(x, fo, ao, eg, eb, wo, g1, b1, w1, w2, g2, b2, first, alpha):
    bsz, seq, d = x.shape
    tm = min(ROW_TILE, seq)
    dff = w1.shape[1]
    row = lambda width: pl.BlockSpec((None, tm, width), lambda b, t: (b, t, 0))
    vec = _const_spec((1, d))
    return pl.pallas_call(
        functools.partial(_post_body, first, alpha),
        grid=(bsz, seq // tm),
        in_specs=[row(d), row(FOURIER_WIDTH), row(ATTN_WIDTH), vec, vec,
                  _const_spec((d, d)), vec, vec, _const_spec((d, dff)), _const_spec((dff, d)), vec, vec],
        out_specs=row(d),
        out_shape=jax.ShapeDtypeStruct((bsz, seq, d), F32),
        compiler_params=pltpu.CompilerParams(vmem_limit_bytes=VMEM_LIMIT),
        name="mix_mlp",
    )(x, fo, ao, eg, eb, wo, g1, b1, w1, w2, g2, b2)


def _trunk(x, consts, params):
    cm, a1, a2, bias = consts
    depth = params['w_in'].shape[0]
    alpha = (2.0 * depth) ** 0.25
    eg = params['ln_emb_g'].reshape(1, -1)
    eb = params['ln_emb_b'].reshape(1, -1)
    for l in range(depth):
        first = l == 0
        w_in = params['w_in'][l]
        c2 = FOURIER_WIDTH + ATTN_WIDTH
        kcols = [w_in[:, c2 + h * HEAD_DIM:c2 + (h + 1) * HEAD_DIM] for h in range(N_KV)]
        w_ext = jnp.concatenate([w_in[:, :c2]] + [kcols[0]] * 2 + [kcols[1]] * 2
                                + [w_in[:, c2 + KV_WIDTH:]], axis=1).astype(BF16)
        sink_row = jnp.repeat(params['sink_logits'][l].astype(F32) * LOG2E, WINDOW)
        sink_row = sink_row.reshape(N_KV, 1, Q_PER_KV * WINDOW)
        u, q, k2, vt = _proj(x, eg, eb, w_ext, first)
        y = _f1(u, a1)
        fo = _f2(y, a2, cm, params['w_fourier'][l].astype(BF16))
        ao = _attn(q, k2, vt, bias, sink_row)
        x = _post(x, fo, ao, eg, eb,
                  params['w_out'][l].astype(BF16),
                  params['ln_mix_g'][l].reshape(1, -1), params['ln_mix_b'][l].reshape(1, -1),
                  params['w_ff1'][l].astype(BF16), params['w_ff2'][l].astype(BF16),
                  params['ln_ffn_g'][l].reshape(1, -1), params['ln_ffn_b'][l].reshape(1, -1),
                  first, alpha)
    return x


def kernel(x_prompt, x_sample, ln_emb_g, ln_emb_b, w_in, w_fourier, sink_logits, w_out,
           ln_mix_g, ln_mix_b, w_ff1, w_ff2, ln_ffn_g, ln_ffn_b):
    params = dict(ln_emb_g=ln_emb_g, ln_emb_b=ln_emb_b, w_in=w_in, w_fourier=w_fourier,
                  sink_logits=sink_logits, w_out=w_out, ln_mix_g=ln_mix_g, ln_mix_b=ln_mix_b,
                  w_ff1=w_ff1, w_ff2=w_ff2, ln_ffn_g=ln_ffn_g, ln_ffn_b=ln_ffn_b)
    cm = _channel_dft()
    a1 = _stage1_matrix()
    bias = _attn_bias()
    outs = []
    for x in (x_prompt, x_sample):
        a2 = _stage2_matrices(x.shape[1])
        outs.append(_trunk(x, (cm, a1, a2, bias), params))
    return tuple(outs)
```

```python
import functools
import math

import numpy as np
import jax
import jax.numpy as jnp
from jax import lax
from jax.experimental import pallas as pl
from jax.experimental.pallas import tpu as pltpu

F32 = jnp.float32
BF16 = jnp.bfloat16

FOURIER_WIDTH = 512
N_GROUPS = 4
GROUP = 128
HEAD_DIM = 64
N_HEADS = 8
N_KV = 2
Q_PER_KV = 4
ATTN_WIDTH = 512
KV_WIDTH = 128
WINDOW = 128
LN_EPS = 1e-5
NEG_BIG = -1e30
LOG2E = math.log2(math.e)
SUBLANES = 8
VMEM_LIMIT = 56 * 1024 * 1024

ROW_TILE = 512
PROJ_TILE = 1024
Q_TILE = 1024
FF_CHUNK = 1024
ATTN_ITEM_HEADS = 2
ATTN_LOOKAHEAD = 4


DFT1 = 256
F1_LANES = 8192
F2_ROWS = 2048


def _split(seq):
    s2 = seq // DFT1
    assert s2 * DFT1 == seq and s2 % SUBLANES == 0
    return s2


def _const_spec(shape):
    n = len(shape)
    return pl.BlockSpec(shape, lambda *_: (0,) * n, pipeline_mode=pl.Buffered(1))


def _layer_norm(x, g, b):
    mu = jnp.mean(x, axis=-1, keepdims=True)
    xc = x - mu
    var = jnp.mean(xc * xc, axis=-1, keepdims=True)
    return xc * lax.rsqrt(var + LN_EPS) * g + b


def _fold_rows(x, op):
    r = x.shape[0]
    while r % (2 * SUBLANES) == 0:
        x = op(x[:r // 2], x[r // 2:])
        r //= 2
    return x


def _channel_dft():
    k = np.arange(GROUP)
    ang = 2.0 * np.pi * np.outer(k, k) / GROUP
    cs = np.concatenate([np.cos(ang), np.sin(ang)], axis=0) / np.sqrt(GROUP)
    return jnp.asarray(cs, F32).astype(BF16)


def _stage1_matrix():
    k = np.arange(DFT1)
    ang = 2.0 * np.pi * np.outer(k, k) / DFT1
    a = np.concatenate([np.cos(ang), -np.sin(ang)], axis=0) / np.sqrt(DFT1)
    return jnp.asarray(a, F32).astype(BF16)


def _stage2_matrices(seq):
    s2 = _split(seq)
    r = SUBLANES * s2
    i32 = jnp.int32
    col = jnp.arange(2 * r, dtype=i32)
    part = col // r
    n2c = (col % r) // SUBLANES
    cb = col % SUBLANES
    k2 = jnp.arange(s2, dtype=i32)
    anga = ((k2[:, None] * n2c[None, :]) % s2).astype(F32) * (2.0 * math.pi / s2)
    ca = jnp.cos(anga) * (1.0 / math.sqrt(s2))
    sa = jnp.sin(anga) * (1.0 / math.sqrt(s2))
    pm = jnp.stack([jnp.where(part == 0, ca, sa), jnp.where(part == 0, -sa, ca)])
    qm = jnp.stack([jnp.where(part == 0, -sa, ca), jnp.where(part == 0, -ca, -sa)])
    k1 = jnp.arange(DFT1, dtype=i32)
    angb = ((k1[:, None] * n2c[None, :]) % seq).astype(F32) * (2.0 * math.pi / seq)
    keep = cb[None, :] == (k1 % SUBLANES)[:, None]
    ni = DFT1 // SUBLANES
    cbm = jnp.where(keep, jnp.cos(angb), 0.0).reshape(ni, SUBLANES, 2 * r)
    sbm = jnp.where(keep, jnp.sin(angb), 0.0).reshape(ni, SUBLANES, 2 * r)
    pmx = jnp.broadcast_to(pm[:, :, None, :], (2, s2, SUBLANES, 2 * r)).reshape(2 * r, 2 * r)
    qmx = jnp.broadcast_to(qm[:, :, None, :], (2, s2, SUBLANES, 2 * r)).reshape(2 * r, 2 * r)

    def body(pm_ref, qm_ref, cb_ref, sb_ref, o_ref):
        tiles = (2 * r) // SUBLANES
        pm3 = pm_ref[...].reshape(tiles, SUBLANES, 2 * r)
        qm3 = qm_ref[...].reshape(tiles, SUBLANES, 2 * r)
        val = pm3 * cb_ref[...][None] + qm3 * sb_ref[...][None]
        o_ref[...] = val.reshape(2 * r, 2 * r).astype(BF16)

    item = pl.BlockSpec((None, SUBLANES, 2 * r), lambda i: (i, 0, 0))
    return pl.pallas_call(
        body,
        grid=(ni,),
        in_specs=[_const_spec((2 * r, 2 * r)), _const_spec((2 * r, 2 * r)), item, item],
        out_specs=pl.BlockSpec((None, 2 * r, 2 * r), lambda i: (i, 0, 0)),
        out_shape=jax.ShapeDtypeStruct((ni, 2 * r, 2 * r), BF16),
        compiler_params=pltpu.CompilerParams(vmem_limit_bytes=VMEM_LIMIT),
        name="stage2_matrices",
    )(pmx, qmx, cbm, sbm)


def _attn_bias():
    q = np.arange(WINDOW)[None, :]
    s = np.arange(3 * WINDOW)[:, None]
    rel = s - WINDOW - q
    slopes = 2.0 ** (-8.0 * (np.arange(N_HEADS) + 1.0) / N_HEADS)
    out = np.empty((N_KV, 3 * WINDOW, Q_PER_KV * WINDOW), np.float32)
    for h in range(N_HEADS):
        b = np.where(np.abs(rel) <= WINDOW, -slopes[h] * np.abs(rel) * LOG2E, NEG_BIG)
        kh, g = divmod(h, Q_PER_KV)
        out[kh, :, g * WINDOW:(g + 1) * WINDOW] = b
    return jnp.asarray(out)


def _proj_body(first, s2, x_ref, eg_ref, eb_ref, w_ref, u_ref, q_ref, k_ref, vt_ref, us_scr):
    x = x_ref[...]
    if first:
        x = _layer_norm(x, eg_ref[...], eb_ref[...])
    p = jnp.dot(x.astype(BF16), w_ref[...], preferred_element_type=F32)
    c1 = FOURIER_WIDTH
    c2 = c1 + ATTN_WIDTH
    c3 = c2 + 2 * KV_WIDTH
    q_ref[...] = (p[:, c1:c2] * (LOG2E / math.sqrt(HEAD_DIM))).astype(BF16)
    k_ref[...] = p[:, c2:c3].astype(BF16)
    vt_ref[...] = p[:, c3:].T.astype(BF16)
    nrow = x_ref.shape[0] // s2
    pitch = s2 + SUBLANES
    for g in range(N_GROUPS):
        for n1 in range(nrow):
            us_scr[g, n1 * pitch:n1 * pitch + s2, :] = p[n1 * s2:(n1 + 1) * s2, g * GROUP:(g + 1) * GROUP]
    for n2 in range(s2):
        for g in range(N_GROUPS):
            piece = us_scr[g, pl.ds(n2, nrow, stride=pitch), :]
            lo = n2 * FOURIER_WIDTH + g * GROUP
            u_ref[:, lo:lo + GROUP] = piece.astype(BF16)


def _proj(x, eg, eb, w_ext, first):
    bsz, seq, d = x.shape
    s2 = _split(seq)
    tm = max(PROJ_TILE, 16 * s2)
    wn = w_ext.shape[1]
    row = lambda width: pl.BlockSpec((None, tm, width), lambda b, t: (b, t, 0))
    return pl.pallas_call(
        functools.partial(_proj_body, first, s2),
        grid=(bsz, seq // tm),
        in_specs=[row(d), _const_spec((1, d)), _const_spec((1, d)), _const_spec((d, wn))],
        out_specs=[pl.BlockSpec((None, tm // s2, s2 * FOURIER_WIDTH), lambda b, t: (b, t, 0)),
                   row(ATTN_WIDTH), row(2 * KV_WIDTH),
                   pl.BlockSpec((None, KV_WIDTH, tm), lambda b, t: (b, 0, t))],
        out_shape=[jax.ShapeDtypeStruct((bsz, DFT1, s2 * FOURIER_WIDTH), BF16),
                   jax.ShapeDtypeStruct((bsz, seq, ATTN_WIDTH), BF16),
                   jax.ShapeDtypeStruct((bsz, seq, 2 * KV_WIDTH), BF16),
                   jax.ShapeDtypeStruct((bsz, KV_WIDTH, seq), BF16)],
        scratch_shapes=[pltpu.VMEM((N_GROUPS, (tm // s2) * (s2 + SUBLANES), GROUP), F32)],
        compiler_params=pltpu.CompilerParams(vmem_limit_bytes=VMEM_LIMIT),
        name="proj",
    )(x, eg, eb, w_ext)


def _f1_body(u_ref, a1_ref, y_ref):
    y = jnp.dot(a1_ref[...], u_ref[...], preferred_element_type=F32).astype(BF16)
    y_ref[0] = y[:DFT1]
    y_ref[1] = y[DFT1:]


def _f1(u2, a1):
    bsz, _, lanes = u2.shape
    lb = min(F1_LANES, lanes)
    return pl.pallas_call(
        _f1_body,
        grid=(bsz, lanes // lb),
        in_specs=[pl.BlockSpec((None, DFT1, lb), lambda b, j: (b, 0, j)), _const_spec(a1.shape)],
        out_specs=pl.BlockSpec((None, 2, DFT1, lb), lambda b, j: (b, 0, 0, j)),
        out_shape=jax.ShapeDtypeStruct((bsz, 2, DFT1, lanes), BF16),
        compiler_params=pltpu.CompilerParams(vmem_limit_bytes=VMEM_LIMIT),
        name="fourier_stage1",
    )(u2, a1)


def _f2_body(nitem, y_ref, a2_ref, cm_ref, wf_ref, o_ref):
    width = FOURIER_WIDTH
    s2 = y_ref.shape[2] // width
    r = SUBLANES * s2
    pr, pi = [], []
    yf = y_ref[...].astype(F32)
    for c in range(nitem):
        rs = slice(c * SUBLANES, (c + 1) * SUBLANES)
        ys = jnp.concatenate([yf[part, rs, n2 * width:(n2 + 1) * width]
                              for part in range(2) for n2 in range(s2)], axis=0).astype(BF16)
        p = jnp.dot(a2_ref[c], ys, preferred_element_type=F32)
        pr.append(p[:r])
        pi.append(p[r:])
    pr = jnp.concatenate(pr, axis=0).astype(BF16)
    pi = jnp.concatenate(pi, axis=0).astype(BF16)
    rows = nitem * r
    both = jnp.concatenate(
        [jnp.concatenate([pr[:, g * GROUP:(g + 1) * GROUP], pi[:, g * GROUP:(g + 1) * GROUP]], axis=1)
         for g in range(N_GROUPS)], axis=0)
    f = jnp.dot(both, cm_ref[...], preferred_element_type=F32).astype(BF16)
    outs = [jnp.dot(f[g * rows:(g + 1) * rows], wf_ref[g], preferred_element_type=F32)
            for g in range(N_GROUPS)]
    out = jnp.concatenate(outs, axis=1)
    for c in range(nitem):
        o_ref[:, c * SUBLANES:(c + 1) * SUBLANES, :] = out[c * r:(c + 1) * r].reshape(s2, SUBLANES, FOURIER_WIDTH)


def _f2(y, a2, cm, wf):
    bsz, _, _, lanes = y.shape
    width = FOURIER_WIDTH
    s2 = lanes // width
    seq = DFT1 * s2
    r = SUBLANES * s2
    nitem = max(2, F2_ROWS // r)
    nm = DFT1 // (SUBLANES * nitem)
    out = pl.pallas_call(
        functools.partial(_f2_body, nitem),
        grid=(nm, bsz),
        in_specs=[pl.BlockSpec((None, 2, nitem * SUBLANES, lanes), lambda m, b: (b, 0, m, 0)),
                  pl.BlockSpec((nitem, 2 * r, 2 * r), lambda m, b: (m, 0, 0)),
                  _const_spec(cm.shape), _const_spec(wf.shape)],
        out_specs=pl.BlockSpec((None, s2, None, nitem * SUBLANES, width), lambda m, b: (b, 0, m, 0, 0)),
        out_shape=jax.ShapeDtypeStruct((bsz, s2, nm, nitem * SUBLANES, width), F32),
        compiler_params=pltpu.CompilerParams(vmem_limit_bytes=VMEM_LIMIT),
        name="fourier_stage2",
    )(y, a2, cm, wf)
    return out.reshape(bsz, seq, width)


def _attn_body(nsub, q_ref, kp_ref, kc_ref, kn_ref, vp_ref, vc_ref, vn_ref, bias_ref, sink_ref, o_ref):
    t = pl.program_id(1)
    last = pl.num_programs(1) - 1
    kcat = jnp.concatenate([kp_ref[...], kc_ref[...], kn_ref[...]], axis=0)
    vtcat = jnp.concatenate([vp_ref[...], vc_ref[...], vn_ref[...]], axis=1)
    neg_first = jnp.where(t == 0, NEG_BIG, 0.0).astype(F32)
    neg_last = jnp.where(t == last, NEG_BIG, 0.0).astype(F32)
    lane = lax.broadcasted_iota(jnp.int32, (WINDOW, 2 * HEAD_DIM), 1)
    m_left = (lane < HEAD_DIM).astype(BF16)
    m_right = 1 - m_left
    pairs_per_item = ATTN_ITEM_HEADS // 2
    items = [(s, kh, p0) for s in range(nsub) for kh in range(N_KV)
             for p0 in range(0, Q_PER_KV // 2, pairs_per_item)]
    width = ATTN_ITEM_HEADS * WINDOW

    def scores(s, kh, p0):
        r0 = s * WINDOW
        parts = []
        for pair in range(p0, p0 + pairs_per_item):
            c0 = (kh * 2 + pair) * GROUP
            qa = q_ref[r0:r0 + WINDOW, c0:c0 + GROUP]
            parts += [qa * m_left, qa * m_right]
        st = jnp.concatenate(parts, axis=0)
        ks = kcat[r0:r0 + 3 * WINDOW, kh * GROUP:(kh + 1) * GROUP]
        sc = lax.dot_general(ks, st, (((1,), (1,)), ((), ())), preferred_element_type=F32)
        sc = sc + bias_ref[kh, :, 2 * p0 * WINDOW:2 * p0 * WINDOW + width]
        if s == 0:
            sc = jnp.concatenate([sc[:WINDOW] + neg_first, sc[WINDOW:]], axis=0)
        if s == nsub - 1:
            sc = jnp.concatenate([sc[:2 * WINDOW], sc[2 * WINDOW:] + neg_last], axis=0)
        return sc

    def softmax(s, kh, p0, sc):
        sink = sink_ref[kh, :, 2 * p0 * WINDOW:2 * p0 * WINDOW + width]
        m = jnp.maximum(jnp.max(_fold_rows(sc, jnp.maximum), axis=0, keepdims=True), sink)
        p = jnp.exp2(sc - m)
        denom = jnp.sum(_fold_rows(p, jnp.add), axis=0, keepdims=True) + jnp.exp2(sink - m)
        return p.astype(BF16), 1.0 / denom

    def weighted_values(s, kh, p0, pn_inv):
        pn, inv = pn_inv
        r0 = s * WINDOW
        vts = vtcat[kh * HEAD_DIM:(kh + 1) * HEAD_DIM, r0:r0 + 3 * WINDOW]
        ot = jnp.dot(vts, pn, preferred_element_type=F32) * inv
        for j in range(pairs_per_item):
            c0 = (kh * 2 + p0 + j) * GROUP
            two = jnp.concatenate([ot[:, 2 * j * WINDOW:(2 * j + 1) * WINDOW],
                                   ot[:, (2 * j + 1) * WINDOW:(2 * j + 2) * WINDOW]], axis=0)
            o_ref[r0:r0 + WINDOW, c0:c0 + GROUP] = two.T

    n = len(items)
    sc = {i: scores(*items[i]) for i in range(min(ATTN_LOOKAHEAD, n))}
    pn_prev = None
    for i in range(n):
        if i + ATTN_LOOKAHEAD < n:
            sc[i + ATTN_LOOKAHEAD] = scores(*items[i + ATTN_LOOKAHEAD])
        pn_cur = softmax(*items[i], sc.pop(i))
        if i >= 1:
            weighted_values(*items[i - 1], pn_prev)
        pn_prev = pn_cur
    weighted_values(*items[n - 1], pn_prev)


def _attn(q, k2, vt, bias, sink_row):
    bsz, seq, _ = q.shape
    tq = min(Q_TILE, seq)
    nsub = tq // WINDOW
    nblk = seq // WINDOW
    kvw = 2 * KV_WIDTH
    kprev = pl.BlockSpec((None, WINDOW, kvw), lambda b, t: (b, jnp.maximum(t * nsub - 1, 0), 0))
    kcur = pl.BlockSpec((None, tq, kvw), lambda b, t: (b, t, 0))
    knext = pl.BlockSpec((None, WINDOW, kvw), lambda b, t: (b, jnp.minimum((t + 1) * nsub, nblk - 1), 0))
    vprev = pl.BlockSpec((None, KV_WIDTH, WINDOW), lambda b, t: (b, 0, jnp.maximum(t * nsub - 1, 0)))
    vcur = pl.BlockSpec((None, KV_WIDTH, tq), lambda b, t: (b, 0, t))
    vnext = pl.BlockSpec((None, KV_WIDTH, WINDOW), lambda b, t: (b, 0, jnp.minimum((t + 1) * nsub, nblk - 1)))
    return pl.pallas_call(
        functools.partial(_attn_body, nsub),
        grid=(bsz, seq // tq),
        in_specs=[pl.BlockSpec((None, tq, ATTN_WIDTH), lambda b, t: (b, t, 0)),
                  kprev, kcur, knext, vprev, vcur, vnext,
                  _const_spec(bias.shape), _const_spec(sink_row.shape)],
        out_specs=pl.BlockSpec((None, tq, ATTN_WIDTH), lambda b, t: (b, t, 0)),
        out_shape=jax.ShapeDtypeStruct((bsz, seq, ATTN_WIDTH), F32),
        compiler_params=pltpu.CompilerParams(vmem_limit_bytes=VMEM_LIMIT),
        name="window_attention",
    )(q, k2, k2, k2, vt, vt, vt, bias, sink_row)


def _post_body(first, alpha, x_ref, fo_ref, ao_ref, eg_ref, eb_ref, wo_ref, g1_ref, b1_ref,
               w1_ref, w2_ref, g2_ref, b2_ref, o_ref):
    dff = w1_ref.shape[1]
    x = x_ref[...]
    if first:
        x = _layer_norm(x, eg_ref[...], eb_ref[...])
    heads = jnp.concatenate([fo_ref[...], ao_ref[...]], axis=1).astype(BF16)
    mix = jnp.dot(heads, wo_ref[...], preferred_element_type=F32)
    x1 = _layer_norm(alpha * x + mix, g1_ref[...], b1_ref[...])
    x1b = x1.astype(BF16)
    z = None
    for c in range(dff // FF_CHUNK):
        h = jnp.dot(x1b, w1_ref[:, c * FF_CHUNK:(c + 1) * FF_CHUNK], preferred_element_type=F32)
        h = jnp.square(jnp.maximum(h, 0.0)).astype(BF16)
        zc = jnp.dot(h, w2_ref[c * FF_CHUNK:(c + 1) * FF_CHUNK, :], preferred_element_type=F32)
        z = zc if z is None else z + zc
    o_ref[...] = _layer_norm(alpha * x1 + z, g2_ref[...], b2_ref[...])


def _post(x, fo, ao, eg, eb, wo, g1, b1, w1, w2, g2, b2, first, alpha):
    bsz, seq, d = x.shape
    tm = min(ROW_TILE, seq)
    dff = w1.shape[1]
    row = lambda width: pl.BlockSpec((None, tm, width), lambda b, t: (b, t, 0))
    vec = _const_spec((1, d))
    return pl.pallas_call(
        functools.partial(_post_body, first, alpha),
        grid=(bsz, seq // tm),
        in_specs=[row(d), row(FOURIER_WIDTH), row(ATTN_WIDTH), vec, vec,
                  _const_spec((d, d)), vec, vec, _const_spec((d, dff)), _const_spec((dff, d)), vec, vec],
        out_specs=row(d),
        out_shape=jax.ShapeDtypeStruct((bsz, seq, d), F32),
        compiler_params=pltpu.CompilerParams(vmem_limit_bytes=VMEM_LIMIT),
        name="mix_mlp",
    )(x, fo, ao, eg, eb, wo, g1, b1, w1, w2, g2, b2)


def _trunk(x, consts, params):
    cm, a1, a2, bias = consts
    depth = params['w_in'].shape[0]
    alpha = (2.0 * depth) ** 0.25
    eg = params['ln_emb_g'].reshape(1, -1)
    eb = params['ln_emb_b'].reshape(1, -1)
    for l in range(depth):
        first = l == 0
        w_in = params['w_in'][l]
        c2 = FOURIER_WIDTH + ATTN_WIDTH
        kcols = [w_in[:, c2 + h * HEAD_DIM:c2 + (h + 1) * HEAD_DIM] for h in range(N_KV)]
        w_ext = jnp.concatenate([w_in[:, :c2]] + [kcols[0]] * 2 + [kcols[1]] * 2
                                + [w_in[:, c2 + KV_WIDTH:]], axis=1).astype(BF16)
        sink_row = jnp.repeat(params['sink_logits'][l].astype(F32) * LOG2E, WINDOW)
        sink_row = sink_row.reshape(N_KV, 1, Q_PER_KV * WINDOW)
        u, q, k2, vt = _proj(x, eg, eb, w_ext, first)
        y = _f1(u, a1)
        fo = _f2(y, a2, cm, params['w_fourier'][l].astype(BF16))
        ao = _attn(q, k2, vt, bias, sink_row)
        x = _post(x, fo, ao, eg, eb,
                  params['w_out'][l].astype(BF16),
                  params['ln_mix_g'][l].reshape(1, -1), params['ln_mix_b'][l].reshape(1, -1),
                  params['w_ff1'][l].astype(BF16), params['w_ff2'][l].astype(BF16),
                  params['ln_ffn_g'][l].reshape(1, -1), params['ln_ffn_b'][l].reshape(1, -1),
                  first, alpha)
    return x


def kernel(x_prompt, x_sample, ln_emb_g, ln_emb_b, w_in, w_fourier, sink_logits, w_out,
           ln_mix_g, ln_mix_b, w_ff1, w_ff2, ln_ffn_g, ln_ffn_b):
    params = dict(ln_emb_g=ln_emb_g, ln_emb_b=ln_emb_b, w_in=w_in, w_fourier=w_fourier,
                  sink_logits=sink_logits, w_out=w_out, ln_mix_g=ln_mix_g, ln_mix_b=ln_mix_b,
                  w_ff1=w_ff1, w_ff2=w_ff2, ln_ffn_g=ln_ffn_g, ln_ffn_b=ln_ffn_b)
    cm = _channel_dft()
    a1 = _stage1_matrix()
    bias = _attn_bias()
    outs = []
    for x in (x_prompt, x_sample):
        a2 = _stage2_matrices(x.shape[1])
        outs.append(_trunk(x, (cm, a1, a2, bias), params))
    return tuple(outs)
```

```python
import functools
import math

import numpy as np
import jax
import jax.numpy as jnp
from jax import lax
from jax.experimental import pallas as pl
from jax.experimental.pallas import tpu as pltpu

F32 = jnp.float32
BF16 = jnp.bfloat16

FOURIER_WIDTH = 512
N_GROUPS = 4
GROUP = 128
HEAD_DIM = 64
N_HEADS = 8
N_KV = 2
Q_PER_KV = 4
ATTN_WIDTH = 512
KV_WIDTH = 128
WINDOW = 128
LN_EPS = 1e-5
NEG_BIG = -1e30
LOG2E = math.log2(math.e)
SUBLANES = 8
VMEM_LIMIT = 56 * 1024 * 1024

ROW_TILE = 512
PROJ_TILE = 1024
Q_TILE = 2048
FF_CHUNK = 1024
ATTN_ITEM_HEADS = 2
ATTN_LOOKAHEAD = 4


DFT1 = 256
F1_LANES = 8192
F2_ROWS = 2048


def _split(seq):
    s2 = seq // DFT1
    assert s2 * DFT1 == seq and s2 % SUBLANES == 0
    return s2


def _const_spec(shape):
    n = len(shape)
    return pl.BlockSpec(shape, lambda *_: (0,) * n, pipeline_mode=pl.Buffered(1))


def _layer_norm(x, g, b):
    mu = jnp.mean(x, axis=-1, keepdims=True)
    xc = x - mu
    var = jnp.mean(xc * xc, axis=-1, keepdims=True)
    return xc * lax.rsqrt(var + LN_EPS) * g + b


def _fold_rows(x, op):
    r = x.shape[0]
    while r % (2 * SUBLANES) == 0:
        x = op(x[:r // 2], x[r // 2:])
        r //= 2
    return x


def _channel_dft():
    k = np.arange(GROUP)
    ang = 2.0 * np.pi * np.outer(k, k) / GROUP
    cs = np.concatenate([np.cos(ang), np.sin(ang)], axis=0) / np.sqrt(GROUP)
    return jnp.asarray(cs, F32).astype(BF16)


def _stage1_matrix():
    k = np.arange(DFT1)
    ang = 2.0 * np.pi * np.outer(k, k) / DFT1
    a = np.concatenate([np.cos(ang), -np.sin(ang)], axis=0) / np.sqrt(DFT1)
    return jnp.asarray(a, F32).astype(BF16)


def _stage2_matrices(seq):
    s2 = _split(seq)
    r = SUBLANES * s2
    i32 = jnp.int32
    col = jnp.arange(2 * r, dtype=i32)
    part = col // r
    n2c = (col % r) // SUBLANES
    cb = col % SUBLANES
    k2 = jnp.arange(s2, dtype=i32)
    anga = ((k2[:, None] * n2c[None, :]) % s2).astype(F32) * (2.0 * math.pi / s2)
    ca = jnp.cos(anga) * (1.0 / math.sqrt(s2))
    sa = jnp.sin(anga) * (1.0 / math.sqrt(s2))
    pm = jnp.stack([jnp.where(part == 0, ca, sa), jnp.where(part == 0, -sa, ca)])
    qm = jnp.stack([jnp.where(part == 0, -sa, ca), jnp.where(part == 0, -ca, -sa)])
    k1 = jnp.arange(DFT1, dtype=i32)
    angb = ((k1[:, None] * n2c[None, :]) % seq).astype(F32) * (2.0 * math.pi / seq)
    keep = cb[None, :] == (k1 % SUBLANES)[:, None]
    ni = DFT1 // SUBLANES
    cbm = jnp.where(keep, jnp.cos(angb), 0.0).reshape(ni, SUBLANES, 2 * r)
    sbm = jnp.where(keep, jnp.sin(angb), 0.0).reshape(ni, SUBLANES, 2 * r)
    pmx = jnp.broadcast_to(pm[:, :, None, :], (2, s2, SUBLANES, 2 * r)).reshape(2 * r, 2 * r)
    qmx = jnp.broadcast_to(qm[:, :, None, :], (2, s2, SUBLANES, 2 * r)).reshape(2 * r, 2 * r)

    def body(pm_ref, qm_ref, cb_ref, sb_ref, o_ref):
        tiles = (2 * r) // SUBLANES
        pm3 = pm_ref[...].reshape(tiles, SUBLANES, 2 * r)
        qm3 = qm_ref[...].reshape(tiles, SUBLANES, 2 * r)
        val = pm3 * cb_ref[...][None] + qm3 * sb_ref[...][None]
        o_ref[...] = val.reshape(2 * r, 2 * r).astype(BF16)

    item = pl.BlockSpec((None, SUBLANES, 2 * r), lambda i: (i, 0, 0))
    return pl.pallas_call(
        body,
        grid=(ni,),
        in_specs=[_const_spec((2 * r, 2 * r)), _const_spec((2 * r, 2 * r)), item, item],
        out_specs=pl.BlockSpec((None, 2 * r, 2 * r), lambda i: (i, 0, 0)),
        out_shape=jax.ShapeDtypeStruct((ni, 2 * r, 2 * r), BF16),
        compiler_params=pltpu.CompilerParams(vmem_limit_bytes=VMEM_LIMIT),
        name="stage2_matrices",
    )(pmx, qmx, cbm, sbm)


def _attn_bias():
    q = np.arange(WINDOW)[None, :]
    s = np.arange(3 * WINDOW)[:, None]
    rel = s - WINDOW - q
    slopes = 2.0 ** (-8.0 * (np.arange(N_HEADS) + 1.0) / N_HEADS)
    out = np.empty((N_KV, 3 * WINDOW, Q_PER_KV * WINDOW), np.float32)
    for h in range(N_HEADS):
        b = np.where(np.abs(rel) <= WINDOW, -slopes[h] * np.abs(rel) * LOG2E, NEG_BIG)
        kh, g = divmod(h, Q_PER_KV)
        out[kh, :, g * WINDOW:(g + 1) * WINDOW] = b
    return jnp.asarray(out)


def _proj_body(first, s2, x_ref, eg_ref, eb_ref, w_ref, u_ref, q_ref, k_ref, vt_ref, us_scr):
    x = x_ref[...]
    if first:
        x = _layer_norm(x, eg_ref[...], eb_ref[...])
    p = jnp.dot(x.astype(BF16), w_ref[...], preferred_element_type=F32)
    c1 = FOURIER_WIDTH
    c2 = c1 + ATTN_WIDTH
    c3 = c2 + 2 * KV_WIDTH
    q_ref[...] = (p[:, c1:c2] * (LOG2E / math.sqrt(HEAD_DIM))).astype(BF16)
    k_ref[...] = p[:, c2:c3].astype(BF16)
    vt_ref[...] = p[:, c3:].T.astype(BF16)
    nrow = x_ref.shape[0] // s2
    pitch = s2 + SUBLANES
    for g in range(N_GROUPS):
        for n1 in range(nrow):
            us_scr[g, n1 * pitch:n1 * pitch + s2, :] = p[n1 * s2:(n1 + 1) * s2, g * GROUP:(g + 1) * GROUP]
    for n2 in range(s2):
        for g in range(N_GROUPS):
            piece = us_scr[g, pl.ds(n2, nrow, stride=pitch), :]
            lo = n2 * FOURIER_WIDTH + g * GROUP
            u_ref[:, lo:lo + GROUP] = piece.astype(BF16)


def _proj(x, eg, eb, w_ext, first):
    bsz, seq, d = x.shape
    s2 = _split(seq)
    tm = max(PROJ_TILE, 16 * s2)
    wn = w_ext.shape[1]
    row = lambda width: pl.BlockSpec((None, tm, width), lambda b, t: (b, t, 0))
    return pl.pallas_call(
        functools.partial(_proj_body, first, s2),
        grid=(bsz, seq // tm),
        in_specs=[row(d), _const_spec((1, d)), _const_spec((1, d)), _const_spec((d, wn))],
        out_specs=[pl.BlockSpec((None, tm // s2, s2 * FOURIER_WIDTH), lambda b, t: (b, t, 0)),
                   row(ATTN_WIDTH), row(2 * KV_WIDTH),
                   pl.BlockSpec((None, KV_WIDTH, tm), lambda b, t: (b, 0, t))],
        out_shape=[jax.ShapeDtypeStruct((bsz, DFT1, s2 * FOURIER_WIDTH), BF16),
                   jax.ShapeDtypeStruct((bsz, seq, ATTN_WIDTH), BF16),
                   jax.ShapeDtypeStruct((bsz, seq, 2 * KV_WIDTH), BF16),
                   jax.ShapeDtypeStruct((bsz, KV_WIDTH, seq), BF16)],
        scratch_shapes=[pltpu.VMEM((N_GROUPS, (tm // s2) * (s2 + SUBLANES), GROUP), F32)],
        compiler_params=pltpu.CompilerParams(vmem_limit_bytes=VMEM_LIMIT),
        name="proj",
    )(x, eg, eb, w_ext)


def _f1_body(u_ref, a1_ref, y_ref):
    y = jnp.dot(a1_ref[...], u_ref[...], preferred_element_type=F32).astype(BF16)
    y_ref[0] = y[:DFT1]
    y_ref[1] = y[DFT1:]


def _f1(u2, a1):
    bsz, _, lanes = u2.shape
    lb = min(F1_LANES, lanes)
    return pl.pallas_call(
        _f1_body,
        grid=(bsz, lanes // lb),
        in_specs=[pl.BlockSpec((None, DFT1, lb), lambda b, j: (b, 0, j)), _const_spec(a1.shape)],
        out_specs=pl.BlockSpec((None, 2, DFT1, lb), lambda b, j: (b, 0, 0, j)),
        out_shape=jax.ShapeDtypeStruct((bsz, 2, DFT1, lanes), BF16),
        compiler_params=pltpu.CompilerParams(vmem_limit_bytes=VMEM_LIMIT),
        name="fourier_stage1",
    )(u2, a1)


def _f2_body(nitem, y_ref, a2_ref, cm_ref, wf_ref, o_ref):
    width = FOURIER_WIDTH
    s2 = y_ref.shape[2] // width
    r = SUBLANES * s2
    pr, pi = [], []
    yf = y_ref[...].astype(F32)
    for c in range(nitem):
        rs = slice(c * SUBLANES, (c + 1) * SUBLANES)
        ys = jnp.concatenate([yf[part, rs, n2 * width:(n2 + 1) * width]
                              for part in range(2) for n2 in range(s2)], axis=0).astype(BF16)
        p = jnp.dot(a2_ref[c], ys, preferred_element_type=F32)
        pr.append(p[:r])
        pi.append(p[r:])
    pr = jnp.concatenate(pr, axis=0).astype(BF16)
    pi = jnp.concatenate(pi, axis=0).astype(BF16)
    rows = nitem * r
    both = jnp.concatenate(
        [jnp.concatenate([pr[:, g * GROUP:(g + 1) * GROUP], pi[:, g * GROUP:(g + 1) * GROUP]], axis=1)
         for g in range(N_GROUPS)], axis=0)
    f = jnp.dot(both, cm_ref[...], preferred_element_type=F32).astype(BF16)
    outs = [jnp.dot(f[g * rows:(g + 1) * rows], wf_ref[g], preferred_element_type=F32)
            for g in range(N_GROUPS)]
    out = jnp.concatenate(outs, axis=1)
    for c in range(nitem):
        o_ref[:, c * SUBLANES:(c + 1) * SUBLANES, :] = out[c * r:(c + 1) * r].reshape(s2, SUBLANES, FOURIER_WIDTH)


def _f2(y, a2, cm, wf):
    bsz, _, _, lanes = y.shape
    width = FOURIER_WIDTH
    s2 = lanes // width
    seq = DFT1 * s2
    r = SUBLANES * s2
    nitem = max(2, F2_ROWS // r)
    nm = DFT1 // (SUBLANES * nitem)
    out = pl.pallas_call(
        functools.partial(_f2_body, nitem),
        grid=(nm, bsz),
        in_specs=[pl.BlockSpec((None, 2, nitem * SUBLANES, lanes), lambda m, b: (b, 0, m, 0)),
                  pl.BlockSpec((nitem, 2 * r, 2 * r), lambda m, b: (m, 0, 0)),
                  _const_spec(cm.shape), _const_spec(wf.shape)],
        out_specs=pl.BlockSpec((None, s2, None, nitem * SUBLANES, width), lambda m, b: (b, 0, m, 0, 0)),
        out_shape=jax.ShapeDtypeStruct((bsz, s2, nm, nitem * SUBLANES, width), F32),
        compiler_params=pltpu.CompilerParams(vmem_limit_bytes=VMEM_LIMIT),
        name="fourier_stage2",
    )(y, a2, cm, wf)
    return out.reshape(bsz, seq, width)


def _attn_body(nsub, q_ref, kp_ref, kc_ref, kn_ref, vp_ref, vc_ref, vn_ref, bias_ref, sink_ref, o_ref):
    t = pl.program_id(1)
    last = pl.num_programs(1) - 1
    kcat = jnp.concatenate([kp_ref[...], kc_ref[...], kn_ref[...]], axis=0)
    vtcat = jnp.concatenate([vp_ref[...], vc_ref[...], vn_ref[...]], axis=1)
    neg_first = jnp.where(t == 0, NEG_BIG, 0.0).astype(F32)
    neg_last = jnp.where(t == last, NEG_BIG, 0.0).astype(F32)
    lane = lax.broadcasted_iota(jnp.int32, (WINDOW, 2 * HEAD_DIM), 1)
    m_left = (lane < HEAD_DIM).astype(BF16)
    m_right = 1 - m_left
    pairs_per_item = ATTN_ITEM_HEADS // 2
    items = [(s, kh, p0) for s in range(nsub) for kh in range(N_KV)
             for p0 in range(0, Q_PER_KV // 2, pairs_per_item)]
    width = ATTN_ITEM_HEADS * WINDOW

    def scores(s, kh, p0):
        r0 = s * WINDOW
        parts = []
        for pair in range(p0, p0 + pairs_per_item):
            c0 = (kh * 2 + pair) * GROUP
            qa = q_ref[r0:r0 + WINDOW, c0:c0 + GROUP]
            parts += [qa * m_left, qa * m_right]
        st = jnp.concatenate(parts, axis=0)
        ks = kcat[r0:r0 + 3 * WINDOW, kh * GROUP:(kh + 1) * GROUP]
        sc = lax.dot_general(ks, st, (((1,), (1,)), ((), ())), preferred_element_type=F32)
        sc = sc + bias_ref[kh, :, 2 * p0 * WINDOW:2 * p0 * WINDOW + width]
        if s == 0:
            sc = jnp.concatenate([sc[:WINDOW] + neg_first, sc[WINDOW:]], axis=0)
        if s == nsub - 1:
            sc = jnp.concatenate([sc[:2 * WINDOW], sc[2 * WINDOW:] + neg_last], axis=0)
        return sc

    def softmax(s, kh, p0, sc):
        sink = sink_ref[kh, :, 2 * p0 * WINDOW:2 * p0 * WINDOW + width]
        m = jnp.maximum(jnp.max(_fold_rows(sc, jnp.maximum), axis=0, keepdims=True), sink)
        p = jnp.exp2(sc - m)
        denom = jnp.sum(_fold_rows(p, jnp.add), axis=0, keepdims=True) + jnp.exp2(sink - m)
        return p.astype(BF16), 1.0 / denom

    def weighted_values(s, kh, p0, pn_inv):
        pn, inv = pn_inv
        r0 = s * WINDOW
        vts = vtcat[kh * HEAD_DIM:(kh + 1) * HEAD_DIM, r0:r0 + 3 * WINDOW]
        ot = jnp.dot(vts, pn, preferred_element_type=F32) * inv
        for j in range(pairs_per_item):
            c0 = (kh * 2 + p0 + j) * GROUP
            two = jnp.concatenate([ot[:, 2 * j * WINDOW:(2 * j + 1) * WINDOW],
                                   ot[:, (2 * j + 1) * WINDOW:(2 * j + 2) * WINDOW]], axis=0)
            o_ref[r0:r0 + WINDOW, c0:c0 + GROUP] = two.T

    n = len(items)
    sc = {i: scores(*items[i]) for i in range(min(ATTN_LOOKAHEAD, n))}
    pn_prev = None
    for i in range(n):
        if i + ATTN_LOOKAHEAD < n:
            sc[i + ATTN_LOOKAHEAD] = scores(*items[i + ATTN_LOOKAHEAD])
        pn_cur = softmax(*items[i], sc.pop(i))
        if i >= 1:
            weighted_values(*items[i - 1], pn_prev)
        pn_prev = pn_cur
    weighted_values(*items[n - 1], pn_prev)


def _attn(q, k2, vt, bias, sink_row):
    bsz, seq, _ = q.shape
    tq = min(Q_TILE, seq)
    nsub = tq // WINDOW
    nblk = seq // WINDOW
    kvw = 2 * KV_WIDTH
    kprev = pl.BlockSpec((None, WINDOW, kvw), lambda b, t: (b, jnp.maximum(t * nsub - 1, 0), 0))
    kcur = pl.BlockSpec((None, tq, kvw), lambda b, t: (b, t, 0))
    knext = pl.BlockSpec((None, WINDOW, kvw), lambda b, t: (b, jnp.minimum((t + 1) * nsub, nblk - 1), 0))
    vprev = pl.BlockSpec((None, KV_WIDTH, WINDOW), lambda b, t: (b, 0, jnp.maximum(t * nsub - 1, 0)))
    vcur = pl.BlockSpec((None, KV_WIDTH, tq), lambda b, t: (b, 0, t))
    vnext = pl.BlockSpec((None, KV_WIDTH, WINDOW), lambda b, t: (b, 0, jnp.minimum((t + 1) * nsub, nblk - 1)))
    return pl.pallas_call(
        functools.partial(_attn_body, nsub),
        grid=(bsz, seq // tq),
        in_specs=[pl.BlockSpec((None, tq, ATTN_WIDTH), lambda b, t: (b, t, 0)),
                  kprev, kcur, knext, vprev, vcur, vnext,
                  _const_spec(bias.shape), _const_spec(sink_row.shape)],
        out_specs=pl.BlockSpec((None, tq, ATTN_WIDTH), lambda b, t: (b, t, 0)),
        out_shape=jax.ShapeDtypeStruct((bsz, seq, ATTN_WIDTH), F32),
        compiler_params=pltpu.CompilerParams(vmem_limit_bytes=VMEM_LIMIT),
        name="window_attention",
    )(q, k2, k2, k2, vt, vt, vt, bias, sink_row)


def _post_body(first, alpha, x_ref, fo_ref, ao_ref, eg_ref, eb_ref, wo_ref, g1_ref, b1_ref,
               w1_ref, w2_ref, g2_ref, b2_ref, o_ref):
    dff = w1_ref.shape[1]
    x = x_ref[...]
    if first:
        x = _layer_norm(x, eg_ref[...], eb_ref[...])
    heads = jnp.concatenate([fo_ref[...], ao_ref[...]], axis=1).astype(BF16)
    mix = jnp.dot(heads, wo_ref[...], preferred_element_type=F32)
    x1 = _layer_norm(alpha * x + mix, g1_ref[...], b1_ref[...])
    x1b = x1.astype(BF16)
    z = None
    for c in range(dff // FF_CHUNK):
        h = jnp.dot(x1b, w1_ref[:, c * FF_CHUNK:(c + 1) * FF_CHUNK], preferred_element_type=F32)
        h = jnp.square(jnp.maximum(h, 0.0)).astype(BF16)
        zc = jnp.dot(h, w2_ref[c * FF_CHUNK:(c + 1) * FF_CHUNK, :], preferred_element_type=F32)
        z = zc if z is None else z + zc
    o_ref[...] = _layer_norm(alpha * x1 + z, g2_ref[...], b2_ref[...])


def _post(x, fo, ao, eg, eb, wo, g1, b1, w1, w2, g2, b2, first, alpha):
    bsz, seq, d = x.shape
    tm = min(ROW_TILE, seq)
    dff = w1.shape[1]
    row = lambda width: pl.BlockSpec((None, tm, width), lambda b, t: (b, t, 0))
    vec = _const_spec((1, d))
    return pl.pallas_call(
        functools.partial(_post_body, first, alpha),
        grid=(bsz, seq // tm),
        in_specs=[row(d), row(FOURIER_WIDTH), row(ATTN_WIDTH), vec, vec,
                  _const_spec((d, d)), vec, vec, _const_spec((d, dff)), _const_spec((dff, d)), vec, vec],
        out_specs=row(d),
        out_shape=jax.ShapeDtypeStruct((bsz, seq, d), F32),
        compiler_params=pltpu.CompilerParams(vmem_limit_bytes=VMEM_LIMIT),
        name="mix_mlp",
    )(x, fo, ao, eg, eb, wo, g1, b1, w1, w2, g2, b2)


def _trunk(x, consts, params):
    cm, a1, a2, bias = consts
    depth = params['w_in'].shape[0]
    alpha = (2.0 * depth) ** 0.25
    eg = params['ln_emb_g'].reshape(1, -1)
    eb = params['ln_emb_b'].reshape(1, -1)
    for l in range(depth):
        first = l == 0
        w_in = params['w_in'][l]
        c2 = FOURIER_WIDTH + ATTN_WIDTH
        kcols = [w_in[:, c2 + h * HEAD_DIM:c2 + (h + 1) * HEAD_DIM] for h in range(N_KV)]
        w_ext = jnp.concatenate([w_in[:, :c2]] + [kcols[0]] * 2 + [kcols[1]] * 2
                                + [w_in[:, c2 + KV_WIDTH:]], axis=1).astype(BF16)
        sink_row = jnp.repeat(params['sink_logits'][l].astype(F32) * LOG2E, WINDOW)
        sink_row = sink_row.reshape(N_KV, 1, Q_PER_KV * WINDOW)
        u, q, k2, vt = _proj(x, eg, eb, w_ext, first)
        y = _f1(u, a1)
        fo = _f2(y, a2, cm, params['w_fourier'][l].astype(BF16))
        ao = _attn(q, k2, vt, bias, sink_row)
        x = _post(x, fo, ao, eg, eb,
                  params['w_out'][l].astype(BF16),
                  params['ln_mix_g'][l].reshape(1, -1), params['ln_mix_b'][l].reshape(1, -1),
                  params['w_ff1'][l].astype(BF16), params['w_ff2'][l].astype(BF16),
                  params['ln_ffn_g'][l].reshape(1, -1), params['ln_ffn_b'][l].reshape(1, -1),
                  first, alpha)
    return x


def kernel(x_prompt, x_sample, ln_emb_g, ln_emb_b, w_in, w_fourier, sink_logits, w_out,
           ln_mix_g, ln_mix_b, w_ff1, w_ff2, ln_ffn_g, ln_ffn_b):
    params = dict(ln_emb_g=ln_emb_g, ln_emb_b=ln_emb_b, w_in=w_in, w_fourier=w_fourier,
                  sink_logits=sink_logits, w_out=w_out, ln_mix_g=ln_mix_g, ln_mix_b=ln_mix_b,
                  w_ff1=w_ff1, w_ff2=w_ff2, ln_ffn_g=ln_ffn_g, ln_ffn_b=ln_ffn_b)
    cm = _channel_dft()
    a1 = _stage1_matrix()
    bias = _attn_bias()
    outs = []
    for x in (x_prompt, x_sample):
        a2 = _stage2_matrices(x.shape[1])
        outs.append(_trunk(x, (cm, a1, a2, bias), params))
    return tuple(outs)
```

```python
import functools
import math

import numpy as np
import jax
import jax.numpy as jnp
from jax import lax
from jax.experimental import pallas as pl
from jax.experimental.pallas import tpu as pltpu

F32 = jnp.float32
BF16 = jnp.bfloat16

FOURIER_WIDTH = 512
N_GROUPS = 4
GROUP = 128
HEAD_DIM = 64
N_HEADS = 8
N_KV = 2
Q_PER_KV = 4
ATTN_WIDTH = 512
KV_WIDTH = 128
WINDOW = 128
LN_EPS = 1e-5
NEG_BIG = -1e30
LOG2E = math.log2(math.e)
SUBLANES = 8
BF16_ROWS = 16
VMEM_LIMIT = 56 * 1024 * 1024

ROW_TILE = 512
PROJ_TILE = 1024
Q_TILE = 2048
FF_CHUNK = 1024
ATTN_ITEM_HEADS = 2
ATTN_LOOKAHEAD = 4


DFT1 = 256
F1_LANES = 8192
F2_ROWS = 2048


def _split(seq):
    s2 = seq // DFT1
    assert s2 * DFT1 == seq and s2 % SUBLANES == 0
    return s2


def _const_spec(shape):
    n = len(shape)
    return pl.BlockSpec(shape, lambda *_: (0,) * n, pipeline_mode=pl.Buffered(1))


def _layer_norm(x, g, b):
    mu = jnp.mean(x, axis=-1, keepdims=True)
    xc = x - mu
    var = jnp.mean(xc * xc, axis=-1, keepdims=True)
    return xc * lax.rsqrt(var + LN_EPS) * g + b


def _fold_rows(x, op):
    r = x.shape[0]
    while r % (2 * SUBLANES) == 0:
        x = op(x[:r // 2], x[r // 2:])
        r //= 2
    return x


def _channel_dft():
    k = np.arange(GROUP)
    ang = 2.0 * np.pi * np.outer(k, k) / GROUP
    cs = np.concatenate([np.cos(ang), np.sin(ang)], axis=0) / np.sqrt(GROUP)
    return jnp.asarray(cs, F32).astype(BF16)


def _stage1_matrix():
    k = np.arange(DFT1)
    ang = 2.0 * np.pi * np.outer(k, k) / DFT1
    a = np.concatenate([np.cos(ang), -np.sin(ang)], axis=0) / np.sqrt(DFT1)
    return jnp.asarray(a, F32).astype(BF16)


def _stage2_matrices(seq):
    s2 = _split(seq)
    r = SUBLANES * s2
    i32 = jnp.int32
    col = jnp.arange(2 * r, dtype=i32)
    part = col // r
    n2c = (col % r) // SUBLANES
    cb = col % SUBLANES
    k2 = jnp.arange(s2, dtype=i32)
    anga = ((k2[:, None] * n2c[None, :]) % s2).astype(F32) * (2.0 * math.pi / s2)
    ca = jnp.cos(anga) * (1.0 / math.sqrt(s2))
    sa = jnp.sin(anga) * (1.0 / math.sqrt(s2))
    pm = jnp.stack([jnp.where(part == 0, ca, sa), jnp.where(part == 0, -sa, ca)])
    qm = jnp.stack([jnp.where(part == 0, -sa, ca), jnp.where(part == 0, -ca, -sa)])
    k1 = jnp.arange(DFT1, dtype=i32)
    angb = ((k1[:, None] * n2c[None, :]) % seq).astype(F32) * (2.0 * math.pi / seq)
    keep = cb[None, :] == (k1 % SUBLANES)[:, None]
    ni = DFT1 // SUBLANES
    cbm = jnp.where(keep, jnp.cos(angb), 0.0).reshape(ni, SUBLANES, 2 * r)
    sbm = jnp.where(keep, jnp.sin(angb), 0.0).reshape(ni, SUBLANES, 2 * r)
    pmx = jnp.broadcast_to(pm[:, :, None, :], (2, s2, SUBLANES, 2 * r)).reshape(2 * r, 2 * r)
    qmx = jnp.broadcast_to(qm[:, :, None, :], (2, s2, SUBLANES, 2 * r)).reshape(2 * r, 2 * r)

    def body(pm_ref, qm_ref, cb_ref, sb_ref, o_ref):
        tiles = (2 * r) // SUBLANES
        pm3 = pm_ref[...].reshape(tiles, SUBLANES, 2 * r)
        qm3 = qm_ref[...].reshape(tiles, SUBLANES, 2 * r)
        val = pm3 * cb_ref[...][None] + qm3 * sb_ref[...][None]
        o_ref[...] = val.reshape(2 * r, 2 * r).astype(BF16)

    item = pl.BlockSpec((None, SUBLANES, 2 * r), lambda i: (i, 0, 0))
    return pl.pallas_call(
        body,
        grid=(ni,),
        in_specs=[_const_spec((2 * r, 2 * r)), _const_spec((2 * r, 2 * r)), item, item],
        out_specs=pl.BlockSpec((None, 2 * r, 2 * r), lambda i: (i, 0, 0)),
        out_shape=jax.ShapeDtypeStruct((ni, 2 * r, 2 * r), BF16),
        compiler_params=pltpu.CompilerParams(vmem_limit_bytes=VMEM_LIMIT),
        name="stage2_matrices",
    )(pmx, qmx, cbm, sbm)


def _attn_bias():
    q = np.arange(WINDOW)[None, :]
    s = np.arange(3 * WINDOW)[:, None]
    rel = s - WINDOW - q
    slopes = 2.0 ** (-8.0 * (np.arange(N_HEADS) + 1.0) / N_HEADS)
    out = np.empty((N_KV, 3 * WINDOW, Q_PER_KV * WINDOW), np.float32)
    for h in range(N_HEADS):
        b = np.where(np.abs(rel) <= WINDOW, -slopes[h] * np.abs(rel) * LOG2E, NEG_BIG)
        kh, g = divmod(h, Q_PER_KV)
        out[kh, :, g * WINDOW:(g + 1) * WINDOW] = b
    return jnp.asarray(out)


def _proj_body(first, s2, x_ref, eg_ref, eb_ref, w_ref, u_ref, q_ref, k_ref, vt_ref, us_scr):
    x = x_ref[...]
    if first:
        x = _layer_norm(x, eg_ref[...], eb_ref[...])
    p = jnp.dot(x.astype(BF16), w_ref[...], preferred_element_type=F32)
    c1 = FOURIER_WIDTH
    c2 = c1 + ATTN_WIDTH
    c3 = c2 + 2 * KV_WIDTH
    q_ref[...] = (p[:, c1:c2] * (LOG2E / math.sqrt(HEAD_DIM))).astype(BF16)
    k_ref[...] = p[:, c2:c3].astype(BF16)
    vt_ref[...] = p[:, c3:].T.astype(BF16)
    nrow = x_ref.shape[0] // s2
    pitch = s2 + SUBLANES
    for g in range(N_GROUPS):
        for n1 in range(nrow):
            us_scr[g, n1 * pitch:n1 * pitch + s2, :] = p[n1 * s2:(n1 + 1) * s2, g * GROUP:(g + 1) * GROUP]
    for n2 in range(s2):
        for g in range(N_GROUPS):
            piece = us_scr[g, pl.ds(n2, nrow, stride=pitch), :]
            lo = n2 * FOURIER_WIDTH + g * GROUP
            u_ref[:, lo:lo + GROUP] = piece.astype(BF16)


def _proj(x, eg, eb, w_ext, first):
    bsz, seq, d = x.shape
    s2 = _split(seq)
    tm = max(PROJ_TILE, BF16_ROWS * s2)
    wn = w_ext.shape[1]
    row = lambda width: pl.BlockSpec((None, tm, width), lambda b, t: (b, t, 0))
    return pl.pallas_call(
        functools.partial(_proj_body, first, s2),
        grid=(bsz, seq // tm),
        in_specs=[row(d), _const_spec((1, d)), _const_spec((1, d)), _const_spec((d, wn))],
        out_specs=[pl.BlockSpec((None, tm // s2, s2 * FOURIER_WIDTH), lambda b, t: (b, t, 0)),
                   row(ATTN_WIDTH), row(2 * KV_WIDTH),
                   pl.BlockSpec((None, KV_WIDTH, tm), lambda b, t: (b, 0, t))],
        out_shape=[jax.ShapeDtypeStruct((bsz, DFT1, s2 * FOURIER_WIDTH), BF16),
                   jax.ShapeDtypeStruct((bsz, seq, ATTN_WIDTH), BF16),
                   jax.ShapeDtypeStruct((bsz, seq, 2 * KV_WIDTH), BF16),
                   jax.ShapeDtypeStruct((bsz, KV_WIDTH, seq), BF16)],
        scratch_shapes=[pltpu.VMEM((N_GROUPS, (tm // s2) * (s2 + SUBLANES), GROUP), F32)],
        compiler_params=pltpu.CompilerParams(vmem_limit_bytes=VMEM_LIMIT),
        name="proj",
    )(x, eg, eb, w_ext)


def _f1_body(u_ref, a1_ref, y_ref):
    y = jnp.dot(a1_ref[...], u_ref[...], preferred_element_type=F32).astype(BF16)
    y_ref[0] = y[:DFT1]
    y_ref[1] = y[DFT1:]


def _f1(u2, a1):
    bsz, _, lanes = u2.shape
    lb = min(F1_LANES, lanes)
    return pl.pallas_call(
        _f1_body,
        grid=(bsz, lanes // lb),
        in_specs=[pl.BlockSpec((None, DFT1, lb), lambda b, j: (b, 0, j)), _const_spec(a1.shape)],
        out_specs=pl.BlockSpec((None, 2, DFT1, lb), lambda b, j: (b, 0, 0, j)),
        out_shape=jax.ShapeDtypeStruct((bsz, 2, DFT1, lanes), BF16),
        compiler_params=pltpu.CompilerParams(vmem_limit_bytes=VMEM_LIMIT),
        name="fourier_stage1",
    )(u2, a1)


def _f2_body(nitem, y_ref, a2_ref, cm_ref, wf_ref, o_ref):
    width = FOURIER_WIDTH
    s2 = y_ref.shape[2] // width
    r = SUBLANES * s2
    pr, pi = [], []
    yf = y_ref[...].astype(F32)
    for c in range(nitem):
        rs = slice(c * SUBLANES, (c + 1) * SUBLANES)
        ys = jnp.concatenate([yf[part, rs, n2 * width:(n2 + 1) * width]
                              for part in range(2) for n2 in range(s2)], axis=0).astype(BF16)
        p = jnp.dot(a2_ref[c], ys, preferred_element_type=F32)
        pr.append(p[:r])
        pi.append(p[r:])
    pr = jnp.concatenate(pr, axis=0).astype(BF16)
    pi = jnp.concatenate(pi, axis=0).astype(BF16)
    rows = nitem * r
    both = jnp.concatenate(
        [jnp.concatenate([pr[:, g * GROUP:(g + 1) * GROUP], pi[:, g * GROUP:(g + 1) * GROUP]], axis=1)
         for g in range(N_GROUPS)], axis=0)
    f = jnp.dot(both, cm_ref[...], preferred_element_type=F32).astype(BF16)
    outs = [jnp.dot(f[g * rows:(g + 1) * rows], wf_ref[g], preferred_element_type=F32)
            for g in range(N_GROUPS)]
    out = jnp.concatenate(outs, axis=1)
    for c in range(nitem):
        o_ref[:, c * SUBLANES:(c + 1) * SUBLANES, :] = out[c * r:(c + 1) * r].reshape(s2, SUBLANES, FOURIER_WIDTH)


def _f2(y, a2, cm, wf):
    bsz, _, _, lanes = y.shape
    width = FOURIER_WIDTH
    s2 = lanes // width
    seq = DFT1 * s2
    r = SUBLANES * s2
    nitem = max(BF16_ROWS // SUBLANES, F2_ROWS // r)
    nm = DFT1 // (SUBLANES * nitem)
    out = pl.pallas_call(
        functools.partial(_f2_body, nitem),
        grid=(nm, bsz),
        in_specs=[pl.BlockSpec((None, 2, nitem * SUBLANES, lanes), lambda m, b: (b, 0, m, 0)),
                  pl.BlockSpec((nitem, 2 * r, 2 * r), lambda m, b: (m, 0, 0)),
                  _const_spec(cm.shape), _const_spec(wf.shape)],
        out_specs=pl.BlockSpec((None, s2, None, nitem * SUBLANES, width), lambda m, b: (b, 0, m, 0, 0)),
        out_shape=jax.ShapeDtypeStruct((bsz, s2, nm, nitem * SUBLANES, width), F32),
        compiler_params=pltpu.CompilerParams(vmem_limit_bytes=VMEM_LIMIT),
        name="fourier_stage2",
    )(y, a2, cm, wf)
    return out.reshape(bsz, seq, width)


def _attn_body(nsub, q_ref, kp_ref, kc_ref, kn_ref, vp_ref, vc_ref, vn_ref, bias_ref, sink_ref, o_ref):
    t = pl.program_id(1)
    last = pl.num_programs(1) - 1
    kcat = jnp.concatenate([kp_ref[...], kc_ref[...], kn_ref[...]], axis=0)
    vtcat = jnp.concatenate([vp_ref[...], vc_ref[...], vn_ref[...]], axis=1)
    neg_first = jnp.where(t == 0, NEG_BIG, 0.0).astype(F32)
    neg_last = jnp.where(t == last, NEG_BIG, 0.0).astype(F32)
    lane = lax.broadcasted_iota(jnp.int32, (WINDOW, 2 * HEAD_DIM), 1)
    m_left = (lane < HEAD_DIM).astype(BF16)
    m_right = 1 - m_left
    pairs_per_item = ATTN_ITEM_HEADS // 2
    items = [(s, kh, p0) for s in range(nsub) for kh in range(N_KV)
             for p0 in range(0, Q_PER_KV // 2, pairs_per_item)]
    width = ATTN_ITEM_HEADS * WINDOW

    def scores(s, kh, p0):
        r0 = s * WINDOW
        parts = []
        for pair in range(p0, p0 + pairs_per_item):
            c0 = (kh * 2 + pair) * GROUP
            qa = q_ref[r0:r0 + WINDOW, c0:c0 + GROUP]
            parts += [qa * m_left, qa * m_right]
        st = jnp.concatenate(parts, axis=0)
        ks = kcat[r0:r0 + 3 * WINDOW, kh * GROUP:(kh + 1) * GROUP]
        sc = lax.dot_general(ks, st, (((1,), (1,)), ((), ())), preferred_element_type=F32)
        sc = sc + bias_ref[kh, :, 2 * p0 * WINDOW:2 * p0 * WINDOW + width]
        if s == 0:
            sc = jnp.concatenate([sc[:WINDOW] + neg_first, sc[WINDOW:]], axis=0)
        if s == nsub - 1:
            sc = jnp.concatenate([sc[:2 * WINDOW], sc[2 * WINDOW:] + neg_last], axis=0)
        return sc

    def softmax(s, kh, p0, sc):
        sink = sink_ref[kh, :, 2 * p0 * WINDOW:2 * p0 * WINDOW + width]
        m = jnp.maximum(jnp.max(_fold_rows(sc, jnp.maximum), axis=0, keepdims=True), sink)
        p = jnp.exp2(sc - m)
        denom = jnp.sum(_fold_rows(p, jnp.add), axis=0, keepdims=True) + jnp.exp2(sink - m)
        return p.astype(BF16), 1.0 / denom

    def weighted_values(s, kh, p0, pn_inv):
        pn, inv = pn_inv
        r0 = s * WINDOW
        vts = vtcat[kh * HEAD_DIM:(kh + 1) * HEAD_DIM, r0:r0 + 3 * WINDOW]
        ot = jnp.dot(vts, pn, preferred_element_type=F32) * inv
        for j in range(pairs_per_item):
            c0 = (kh * 2 + p0 + j) * GROUP
            two = jnp.concatenate([ot[:, 2 * j * WINDOW:(2 * j + 1) * WINDOW],
                                   ot[:, (2 * j + 1) * WINDOW:(2 * j + 2) * WINDOW]], axis=0)
            o_ref[r0:r0 + WINDOW, c0:c0 + GROUP] = two.T

    n = len(items)
    sc = {i: scores(*items[i]) for i in range(min(ATTN_LOOKAHEAD, n))}
    pn_prev = None
    for i in range(n):
        if i + ATTN_LOOKAHEAD < n:
            sc[i + ATTN_LOOKAHEAD] = scores(*items[i + ATTN_LOOKAHEAD])
        pn_cur = softmax(*items[i], sc.pop(i))
        if i >= 1:
            weighted_values(*items[i - 1], pn_prev)
        pn_prev = pn_cur
    weighted_values(*items[n - 1], pn_prev)


def _attn(q, k2, vt, bias, sink_row):
    bsz, seq, _ = q.shape
    tq = min(Q_TILE, seq)
    nsub = tq // WINDOW
    nblk = seq // WINDOW
    kvw = 2 * KV_WIDTH
    kprev = pl.BlockSpec((None, WINDOW, kvw), lambda b, t: (b, jnp.maximum(t * nsub - 1, 0), 0))
    kcur = pl.BlockSpec((None, tq, kvw), lambda b, t: (b, t, 0))
    knext = pl.BlockSpec((None, WINDOW, kvw), lambda b, t: (b, jnp.minimum((t + 1) * nsub, nblk - 1), 0))
    vprev = pl.BlockSpec((None, KV_WIDTH, WINDOW), lambda b, t: (b, 0, jnp.maximum(t * nsub - 1, 0)))
    vcur = pl.BlockSpec((None, KV_WIDTH, tq), lambda b, t: (b, 0, t))
    vnext = pl.BlockSpec((None, KV_WIDTH, WINDOW), lambda b, t: (b, 0, jnp.minimum((t + 1) * nsub, nblk - 1)))
    return pl.pallas_call(
        functools.partial(_attn_body, nsub),
        grid=(bsz, seq // tq),
        in_specs=[pl.BlockSpec((None, tq, ATTN_WIDTH), lambda b, t: (b, t, 0)),
                  kprev, kcur, knext, vprev, vcur, vnext,
                  _const_spec(bias.shape), _const_spec(sink_row.shape)],
        out_specs=pl.BlockSpec((None, tq, ATTN_WIDTH), lambda b, t: (b, t, 0)),
        out_shape=jax.ShapeDtypeStruct((bsz, seq, ATTN_WIDTH), F32),
        compiler_params=pltpu.CompilerParams(vmem_limit_bytes=VMEM_LIMIT),
        name="window_attention",
    )(q, k2, k2, k2, vt, vt, vt, bias, sink_row)


def _post_body(first, alpha, x_ref, fo_ref, ao_ref, eg_ref, eb_ref, wo_ref, g1_ref, b1_ref,
               w1_ref, w2_ref, g2_ref, b2_ref, o_ref):
    dff = w1_ref.shape[1]
    x = x_ref[...]
    if first:
        x = _layer_norm(x, eg_ref[...], eb_ref[...])
    heads = jnp.concatenate([fo_ref[...], ao_ref[...]], axis=1).astype(BF16)
    mix = jnp.dot(heads, wo_ref[...], preferred_element_type=F32)
    x1 = _layer_norm(alpha * x + mix, g1_ref[...], b1_ref[...])
    x1b = x1.astype(BF16)
    z = None
    for c in range(dff // FF_CHUNK):
        h = jnp.dot(x1b, w1_ref[:, c * FF_CHUNK:(c + 1) * FF_CHUNK], preferred_element_type=F32)
        h = jnp.square(jnp.maximum(h, 0.0)).astype(BF16)
        zc = jnp.dot(h, w2_ref[c * FF_CHUNK:(c + 1) * FF_CHUNK, :], preferred_element_type=F32)
        z = zc if z is None else z + zc
    o_ref[...] = _layer_norm(alpha * x1 + z, g2_ref[...], b2_ref[...])


def _post(x, fo, ao, eg, eb, wo, g1, b1, w1, w2, g2, b2, first, alpha):
    bsz, seq, d = x.shape
    tm = min(ROW_TILE, seq)
    dff = w1.shape[1]
    row = lambda width: pl.BlockSpec((None, tm, width), lambda b, t: (b, t, 0))
    vec = _const_spec((1, d))
    return pl.pallas_call(
        functools.partial(_post_body, first, alpha),
        grid=(bsz, seq // tm),
        in_specs=[row(d), row(FOURIER_WIDTH), row(ATTN_WIDTH), vec, vec,
                  _const_spec((d, d)), vec, vec, _const_spec((d, dff)), _const_spec((dff, d)), vec, vec],
        out_specs=row(d),
        out_shape=jax.ShapeDtypeStruct((bsz, seq, d), F32),
        compiler_params=pltpu.CompilerParams(vmem_limit_bytes=VMEM_LIMIT),
        name="mix_mlp",
    )(x, fo, ao, eg, eb, wo, g1, b1, w1, w2, g2, b2)


def _trunk(x, consts, params):
    cm, a1, a2, bias = consts
    depth = params['w_in'].shape[0]
    alpha = (2.0 * depth) ** 0.25
    eg = params['ln_emb_g'].reshape(1, -1)
    eb = params['ln_emb_b'].reshape(1, -1)
    for l in range(depth):
        first = l == 0
        w_in = params['w_in'][l]
        c2 = FOURIER_WIDTH + ATTN_WIDTH
        kcols = [w_in[:, c2 + h * HEAD_DIM:c2 + (h + 1) * HEAD_DIM] for h in range(N_KV)]
        w_ext = jnp.concatenate([w_in[:, :c2]] + [kcols[0]] * 2 + [kcols[1]] * 2
                                + [w_in[:, c2 + KV_WIDTH:]], axis=1).astype(BF16)
        sink_row = jnp.repeat(params['sink_logits'][l].astype(F32) * LOG2E, WINDOW)
        sink_row = sink_row.reshape(N_KV, 1, Q_PER_KV * WINDOW)
        u, q, k2, vt = _proj(x, eg, eb, w_ext, first)
        y = _f1(u, a1)
        fo = _f2(y, a2, cm, params['w_fourier'][l].astype(BF16))
        ao = _attn(q, k2, vt, bias, sink_row)
        x = _post(x, fo, ao, eg, eb,
                  params['w_out'][l].astype(BF16),
                  params['ln_mix_g'][l].reshape(1, -1), params['ln_mix_b'][l].reshape(1, -1),
                  params['w_ff1'][l].astype(BF16), params['w_ff2'][l].astype(BF16),
                  params['ln_ffn_g'][l].reshape(1, -1), params['ln_ffn_b'][l].reshape(1, -1),
                  first, alpha)
    return x


def kernel(x_prompt, x_sample, ln_emb_g, ln_emb_b, w_in, w_fourier, sink_logits, w_out,
           ln_mix_g, ln_mix_b, w_ff1, w_ff2, ln_ffn_g, ln_ffn_b):
    params = dict(ln_emb_g=ln_emb_g, ln_emb_b=ln_emb_b, w_in=w_in, w_fourier=w_fourier,
                  sink_logits=sink_logits, w_out=w_out, ln_mix_g=ln_mix_g, ln_mix_b=ln_mix_b,
                  w_ff1=w_ff1, w_ff2=w_ff2, ln_ffn_g=ln_ffn_g, ln_ffn_b=ln_ffn_b)
    cm = _channel_dft()
    a1 = _stage1_matrix()
    bias = _attn_bias()
    outs = []
    for x in (x_prompt, x_sample):
        a2 = _stage2_matrices(x.shape[1])
        outs.append(_trunk(x, (cm, a1, a2, bias), params))
    return tuple(outs)
```
